```python
import math
import jax, jax.numpy as jnp
from jax import lax
import numpy as np

D_MODEL = 2048
BATCH = 1
SEQ = 8192
DEPTH = 2

HEAD_DIM = 128
ATTN_WIDTH = 3 * D_MODEL // 4
ATTN_HEADS = ATTN_WIDTH // HEAD_DIM
LRU_WIDTH = 3 * D_MODEL // 4
LRU_BLOCKS = 12
LRU_BLOCK_W = LRU_WIDTH // LRU_BLOCKS
MIX_WIDTH = ATTN_WIDTH + LRU_WIDTH
IN_WIDTH = 3 * ATTN_WIDTH + 2 * LRU_WIDTH
CONV_WIDTH = 4
LRU_C = 8.0
ROPE_THETA = 10000.0
DILATED_BRANCHES = ((128, 1), (512, 4), (2048, 16))
Q_BLOCK = 128
NEG_INF = -1e30
NORM_EPS = 1e-6
N_EXPERTS = 32
TOP_K = 4
EXPERT_FF = D_MODEL
SWIGLU_LIMIT = 7.0
SWIGLU_ALPHA = 1.702
MOE_BLOCK = 256

kernel_name = 'hybrid_dilated_attn_rglru_moe_block'


def rmsnorm(x, g):
    x32 = x.astype(jnp.float32)
    y = x32 * lax.rsqrt(jnp.mean(x32 * x32, axis=-1, keepdims=True) + NORM_EPS)
    return (y * g.astype(jnp.float32)).astype(x.dtype)


def rope_tables(positions):
    inv_freq = ROPE_THETA ** (-jnp.arange(0, HEAD_DIM, 2, dtype=jnp.float32) / HEAD_DIM)
    ang = positions.astype(jnp.float32)[..., None] * inv_freq
    ang = jnp.concatenate([ang, ang], axis=-1)[:, :, None, :]
    return jnp.cos(ang), jnp.sin(ang)


def apply_rope(t, cos, sin):
    t32 = t.astype(jnp.float32)
    half = HEAD_DIM // 2
    rot = jnp.concatenate([-t32[..., half:], t32[..., :half]], axis=-1)
    return (t32 * cos + rot * sin).astype(t.dtype)


def dilated_branch(q, k, v, window, dilation):
    B, S, H, E = q.shape
    d = dilation
    w = window // d
    L = S // d
    bq = math.gcd(L, Q_BLOCK)
    nb = L // bq

    def to_res(t):
        return t.reshape(B, L, d, H, E).transpose(0, 3, 2, 1, 4)

    qb = to_res(q).reshape(B, H, d, nb, bq, E)
    pad = ((0, 0), (0, 0), (0, 0), (w, 0), (0, 0))
    idx = jnp.arange(nb)[:, None] * bq + jnp.arange(bq + w)[None, :]
    kb = jnp.pad(to_res(k), pad)[:, :, :, idx]
    vb = jnp.pad(to_res(v), pad)[:, :, :, idx]
    s = jnp.einsum('bhrnqe,bhrnke->bhrnqk', qb, kb,
                   preferred_element_type=jnp.float32) * (E ** -0.5)
    qi = jnp.arange(bq)[:, None]
    kk = jnp.arange(bq + w)[None, :]
    band = (kk >= qi) & (kk <= qi + w)
    valid = idx >= w
    mask = band[None, :, :] & valid[:, None, :]
    s = jnp.where(mask, s, NEG_INF)
    m = jnp.max(s, axis=-1, keepdims=True)
    p = jnp.exp(s - m)
    den = jnp.sum(p, axis=-1, keepdims=True)
    o = jnp.einsum('bhrnqk,bhrnke->bhrnqe', p, vb.astype(jnp.float32)) / den
    lse = (m + jnp.log(den))[..., 0]
    o = o.reshape(B, H, d, L, E).transpose(0, 3, 2, 1, 4).reshape(B, S, H, E)
    lse = lse.reshape(B, H, d, L).transpose(0, 3, 2, 1).reshape(B, S, H)
    return o, lse


def mixture_of_dilated_attention(q, k, v):
    outs, lses = [], []
    for window, dilation in DILATED_BRANCHES:
        o, lse = dilated_branch(q, k, v, window, dilation)
        outs.append(o)
        lses.append(lse)
    wts = jax.nn.softmax(jnp.stack(lses, axis=0), axis=0)
    out = jnp.sum(wts[..., None] * jnp.stack(outs, axis=0), axis=0)
    return out.astype(q.dtype)


def causal_depthwise_conv(x, w, b):
    C = x.shape[-1]
    y = lax.conv_general_dilated(x, w[:, None, :].astype(x.dtype), window_strides=(1,),
                                 padding=[(CONV_WIDTH - 1, 0)],
                                 dimension_numbers=('NWC', 'WIO', 'NWC'),
                                 feature_group_count=C)
    return y + b.astype(x.dtype)


def _lin_rec_combine(left, right):
    a_l, b_l = left
    a_r, b_r = right
    return a_l * a_r, a_r * b_l + b_r


def rg_lru(xc, positions, wa, ba, wx, bx, lam):
    B, S, C = xc.shape
    xb = xc.reshape(B, S, LRU_BLOCKS, LRU_BLOCK_W)
    r = jax.nn.sigmoid(jnp.einsum('bshi,hij->bshj', xb, wa).reshape(B, S, C) + ba)
    i = jax.nn.sigmoid(jnp.einsum('bshi,hij->bshj', xb, wx).reshape(B, S, C) + bx)
    log_a = -LRU_C * r.astype(jnp.float32) * jax.nn.softplus(-lam.astype(jnp.float32))
    reset = (positions == 0)[..., None]
    a = jnp.where(reset, 0.0, jnp.exp(log_a))
    mult = jnp.where(reset, 1.0, jnp.sqrt(-jnp.expm1(2.0 * log_a)))
    bterm = xc.astype(jnp.float32) * i.astype(jnp.float32) * mult
    _, h = lax.associative_scan(_lin_rec_combine, (a, bterm), axis=1)
    return h.astype(xc.dtype)


def hybrid_mixer(h, positions, w_in, conv_w, conv_b, wa, ba, wx, bx, lam, attn_g, lru_g, w_out):
    B, S, _ = h.shape
    z = h @ w_in
    q, k, v, xr, gr = jnp.split(
        z, [ATTN_WIDTH, 2 * ATTN_WIDTH, 3 * ATTN_WIDTH, 3 * ATTN_WIDTH + LRU_WIDTH], axis=-1)
    q = q.reshape(B, S, ATTN_HEADS, HEAD_DIM)
    k = k.reshape(B, S, ATTN_HEADS, HEAD_DIM)
    v = v.reshape(B, S, ATTN_HEADS, HEAD_DIM)
    cos, sin = rope_tables(positions)
    q = apply_rope(q, cos, sin)
    k = apply_rope(k, cos, sin)
    attn = mixture_of_dilated_attention(q, k, v).reshape(B, S, ATTN_WIDTH)
    xc = causal_depthwise_conv(xr, conv_w, conv_b)
    lru = rg_lru(xc, positions, wa, ba, wx, bx, lam) * jax.nn.gelu(gr, approximate=True)
    y = jnp.concatenate([rmsnorm(attn, attn_g), rmsnorm(lru, lru_g)], axis=-1)
    return y @ w_out


def moe(h, router_w, router_b, w1, b1, w2, b2, layer):
    B, S, D = h.shape
    T = B * S
    hf = h.reshape(T, D)
    logits = (hf @ router_w).astype(jnp.float32) + router_b.astype(jnp.float32)
    top_vals, top_idx = lax.top_k(logits, TOP_K)
    gates = jax.nn.softmax(top_vals, axis=-1)
    A = T * TOP_K
    e_flat = top_idx.reshape(A)
    tok_flat = jnp.repeat(jnp.arange(T, dtype=jnp.int32), TOP_K)
    g_flat = gates.reshape(A)
    order = jnp.argsort(e_flat)
    e_sorted = e_flat[order]
    counts = jnp.bincount(e_flat, length=N_EXPERTS)
    padded = ((counts + MOE_BLOCK - 1) // MOE_BLOCK) * MOE_BLOCK
    pad_end = jnp.cumsum(padded)
    pad_start = pad_end - padded
    start = jnp.cumsum(counts) - counts
    dest = pad_start[e_sorted] + (jnp.arange(A) - start[e_sorted])
    n_blocks = -(-A // MOE_BLOCK) + N_EXPERTS
    P = n_blocks * MOE_BLOCK
    tok_buf = jnp.zeros((P,), jnp.int32).at[dest].set(tok_flat[order])
    gate_buf = jnp.zeros((P,), jnp.float32).at[dest].set(g_flat[order])
    block_expert = jnp.clip(
        jnp.searchsorted(pad_end, jnp.arange(n_blocks) * MOE_BLOCK, side='right'),
        0, N_EXPERTS - 1)

    def block_fn(args):
        tok, g, e = args
        xb = hf[tok]
        u = xb @ w1[layer, e] + b1[layer, e]
        x_glu = jnp.minimum(u[:, ::2], SWIGLU_LIMIT)
        x_lin = jnp.clip(u[:, 1::2], -SWIGLU_LIMIT, SWIGLU_LIMIT)
        act = x_glu * jax.nn.sigmoid(SWIGLU_ALPHA * x_glu) * (x_lin + 1.0)
        y = act @ w2[layer, e] + b2[layer, e]
        return y.astype(jnp.float32) * g[:, None]

    ys = lax.map(block_fn, (tok_buf.reshape(n_blocks, MOE_BLOCK),
                            gate_buf.reshape(n_blocks, MOE_BLOCK), block_expert))
    out = jnp.zeros((T, D), jnp.float32).at[tok_buf].add(ys.reshape(P, D))
    return out.astype(h.dtype).reshape(B, S, D)


def setup_inputs(seed: int = 0) -> dict:
    key = jax.random.key(seed)
    ks = jax.random.split(key, 24)
    f32 = jnp.float32
    L = DEPTH

    def nrm(k, shape, scale):
        return jax.random.normal(k, shape, f32) * scale

    x = nrm(ks[0], (BATCH, SEQ, D_MODEL), 1.0)
    c = nrm(ks[1], (BATCH, D_MODEL), 1.0)
    positions = jnp.broadcast_to(jnp.arange(SEQ, dtype=jnp.int32), (BATCH, SEQ))
    ada_w = nrm(ks[2], (L, D_MODEL, 6 * D_MODEL), 0.5 * D_MODEL ** -0.5)
    ada_b = nrm(ks[3], (L, 6 * D_MODEL), 0.02)
    norm1_g = 1.0 + nrm(ks[4], (L, D_MODEL), 0.02)
    norm2_g = 1.0 + nrm(ks[5], (L, D_MODEL), 0.02)
    w_in = nrm(ks[6], (L, D_MODEL, IN_WIDTH), D_MODEL ** -0.5)
    conv_w = nrm(ks[7], (L, CONV_WIDTH, LRU_WIDTH), CONV_WIDTH ** -0.5)
    conv_b = nrm(ks[8], (L, LRU_WIDTH), 0.01)
    lru_wa = nrm(ks[9], (L, LRU_BLOCKS, LRU_BLOCK_W, LRU_BLOCK_W), LRU_BLOCK_W ** -0.5)
    lru_ba = nrm(ks[10], (L, LRU_WIDTH), 0.01)
    lru_wx = nrm(ks[11], (L, LRU_BLOCKS, LRU_BLOCK_W, LRU_BLOCK_W), LRU_BLOCK_W ** -0.5)
    lru_bx = nrm(ks[12], (L, LRU_WIDTH), 0.01)
    a_c = jax.random.uniform(ks[13], (L, LRU_WIDTH), f32, 0.9, 0.999)
    s = a_c ** (1.0 / LRU_C)
    lru_lambda = jnp.log(s) - jnp.log1p(-s)
    attn_out_g = 1.0 + nrm(ks[14], (L, ATTN_WIDTH), 0.02)
    lru_out_g = 1.0 + nrm(ks[15], (L, LRU_WIDTH), 0.02)
    w_out = nrm(ks[16], (L, MIX_WIDTH, D_MODEL), MIX_WIDTH ** -0.5)
    router_w = nrm(ks[17], (L, D_MODEL, N_EXPERTS), D_MODEL ** -0.5)
    router_b = nrm(ks[18], (L, N_EXPERTS), 0.01)
    w1 = nrm(ks[19], (L, N_EXPERTS, D_MODEL, 2 * EXPERT_FF), D_MODEL ** -0.5)
    b1 = nrm(ks[20], (L, N_EXPERTS, 2 * EXPERT_FF), 0.01)
    w2 = nrm(ks[21], (L, N_EXPERTS, EXPERT_FF, D_MODEL), EXPERT_FF ** -0.5)
    b2 = nrm(ks[22], (L, N_EXPERTS, D_MODEL), 0.01)
    final_g = 1.0 + nrm(ks[23], (D_MODEL,), 0.02)
    return {'x': x, 'c': c, 'positions': positions, 'ada_w': ada_w, 'ada_b': ada_b,
            'norm1_g': norm1_g, 'norm2_g': norm2_g, 'w_in': w_in, 'conv_w': conv_w,
            'conv_b': conv_b, 'lru_wa': lru_wa, 'lru_ba': lru_ba, 'lru_wx': lru_wx,
            'lru_bx': lru_bx, 'lru_lambda': lru_lambda, 'attn_out_g': attn_out_g,
            'lru_out_g': lru_out_g, 'w_out': w_out, 'router_w': router_w,
            'router_b': router_b, 'w1': w1, 'b1': b1, 'w2': w2, 'b2': b2,
            'final_g': final_g}


def reference(x, c, positions, ada_w, ada_b, norm1_g, norm2_g, w_in, conv_w, conv_b,
              lru_wa, lru_ba, lru_wx, lru_bx, lru_lambda, attn_out_g, lru_out_g, w_out,
              router_w, router_b, w1, b1, w2, b2, final_g):
    cond = jax.nn.silu(c)
    for l in range(DEPTH):
        mod = cond @ ada_w[l] + ada_b[l]
        sh1, sc1, g1, sh2, sc2, g2 = jnp.split(mod, 6, axis=-1)
        h = rmsnorm(x, norm1_g[l]) * (1.0 + sc1[:, None, :]) + sh1[:, None, :]
        y = hybrid_mixer(h, positions, w_in[l], conv_w[l], conv_b[l], lru_wa[l], lru_ba[l],
                         lru_wx[l], lru_bx[l], lru_lambda[l], attn_out_g[l], lru_out_g[l],
                         w_out[l])
        x = x + g1[:, None, :] * y
        h = rmsnorm(x, norm2_g[l]) * (1.0 + sc2[:, None, :]) + sh2[:, None, :]
        y = moe(h, router_w[l], router_b[l], w1, b1, w2, b2, l)
        x = x + g2[:, None, :] * y
    return rmsnorm(x, final_g)
```

```python
import functools

import jax
import jax.numpy as jnp
from jax import lax
from jax.experimental import pallas as pl
from jax.experimental.pallas import tpu as pltpu

F32 = jnp.float32
BF16 = jnp.bfloat16
U32 = jnp.uint32

HEAD_DIM = 128
LRU_BLOCK_W = 128
CONV_WIDTH = 4
LRU_C = 8.0
ROPE_THETA = 10000.0
DILATED_BRANCHES = ((128, 1), (512, 4), (2048, 16))
Q_BLOCK = 128
NEG_INF = -1e30
NORM_EPS = 1e-6
TOP_K = 4
SWIGLU_LIMIT = 7.0
SWIGLU_ALPHA = 1.702

LANES = 128
SUBLANES = 8
VMEM_LIMIT = 56 * 1024 * 1024

MOE_SUB = 256
MOE_UNIT = 1024
MOE_TF = 256
TOK_TILE = 256


def _cparams(sem, vmem=VMEM_LIMIT):
    return pltpu.CompilerParams(dimension_semantics=sem, vmem_limit_bytes=vmem)


def _sigmoid(x):
    return 1.0 / (1.0 + jnp.exp(-x))


def _adaln_kernel(c_ref, w_ref, b_ref, o_ref):
    d, tn = w_ref.shape
    rows = 16

    def body(i, acc):
        r = pl.multiple_of(i * rows, rows)
        cv = c_ref[pl.ds(r, rows), :]
        cv = cv * _sigmoid(cv)
        return acc + w_ref[pl.ds(r, rows), :] * cv

    acc = lax.fori_loop(0, d // rows, body, jnp.zeros((rows, tn), F32))
    o_ref[...] = jnp.sum(acc, axis=0, keepdims=True) + b_ref[...]


def _adaln(c, ada_w, ada_b):
    n_layers, d, n = ada_w.shape
    tn = 1024
    return pl.pallas_call(
        _adaln_kernel,
        grid=(n_layers, n // tn),
        in_specs=[pl.BlockSpec((d, 1), lambda l, j: (0, 0)),
                  pl.BlockSpec((None, d, tn), lambda l, j: (l, 0, j)),
                  pl.BlockSpec((None, 1, tn), lambda l, j: (l, 0, j))],
        out_specs=pl.BlockSpec((None, 1, tn), lambda l, j: (l, 0, j)),
        out_shape=jax.ShapeDtypeStruct((n_layers, 1, n), F32),
        compiler_params=_cparams(("arbitrary", "arbitrary")),
        name="adaln",
    )(c.reshape(d, 1), ada_w, ada_b.reshape(n_layers, 1, n))


def _rope_kernel(pos_ref, invf_ref, cos_ref, sin_ref):
    ang = pos_ref[...].astype(F32) * invf_ref[...]
    cos_ref[...] = jnp.cos(ang)
    s = jnp.sin(ang)
    lane = lax.broadcasted_iota(jnp.int32, s.shape, 1)
    sin_ref[...] = jnp.where(lane < HEAD_DIM // 2, -s, s)


def _rope_tables(pos_col):
    t = pos_col.shape[0]
    tm = min(t, 1024)
    inv = ROPE_THETA ** (-jnp.arange(0, HEAD_DIM, 2, dtype=F32) / HEAD_DIM)
    inv = jnp.concatenate([inv, inv]).reshape(1, HEAD_DIM)
    return pl.pallas_call(
        _rope_kernel,
        grid=(t // tm,),
        in_specs=[pl.BlockSpec((tm, 1), lambda i: (i, 0)),
                  pl.BlockSpec((1, HEAD_DIM), lambda i: (0, 0))],
        out_specs=[pl.BlockSpec((tm, HEAD_DIM), lambda i: (i, 0))] * 2,
        out_shape=[jax.ShapeDtypeStruct((t, HEAD_DIM), F32)] * 2,
        compiler_params=_cparams(("arbitrary",)),
        name="rope_tables",
    )(pos_col, inv)


def _inproj_kernel(x_ref, g_ref, sc_ref, sh_ref, w_ref, cos_ref, sin_ref, o_ref, h_scr, acc_scr,
                   *, q_tiles, rope_tiles):
    j = pl.program_id(1)
    tm, tn = acc_scr.shape
    rows = 32

    @pl.when(j == 0)
    def _():
        a = g_ref[...] * (1.0 + sc_ref[...])
        b = sh_ref[...]

        def body(i, _):
            r = pl.multiple_of(i * rows, rows)
            xv = x_ref[pl.ds(r, rows), :]
            ms = jnp.mean(xv * xv, axis=-1, keepdims=True)
            h_scr[pl.ds(r, rows), :] = (xv * lax.rsqrt(ms + NORM_EPS) * a + b).astype(BF16)
            return 0

        lax.fori_loop(0, tm // rows, body, 0)

    acc_scr[...] = jnp.dot(h_scr[...], w_ref[...], preferred_element_type=F32)

    @pl.when(j < rope_tiles)
    def _():
        scale = jnp.where(j < q_tiles, HEAD_DIM ** -0.5, 1.0).astype(F32)

        def body(i, _):
            r = pl.multiple_of(i * rows, rows)
            cs = cos_ref[pl.ds(r, rows), :] * scale
            sn = sin_ref[pl.ds(r, rows), :] * scale
            for c in range(tn // HEAD_DIM):
                t = acc_scr[pl.ds(r, rows), c * HEAD_DIM:(c + 1) * HEAD_DIM]
                rot = pltpu.roll(t, HEAD_DIM // 2, axis=1)
                o_ref[pl.ds(r, rows), c * HEAD_DIM:(c + 1) * HEAD_DIM] = (t * cs + rot * sn).astype(BF16)
            return 0

        lax.fori_loop(0, tm // rows, body, 0)

    @pl.when(j >= rope_tiles)
    def _():
        o_ref[...] = acc_scr[...].astype(BF16)


def _inproj(x, g, sc, sh, w_bf16, cos, sin, attn_width):
    t, d = x.shape
    n = w_bf16.shape[1]
    tm, tn = min(t, 1024), 512
    kern = functools.partial(_inproj_kernel, q_tiles=attn_width // tn, rope_tiles=2 * attn_width // tn)
    vec = pl.BlockSpec((1, d), lambda i, j: (0, 0))
    return pl.pallas_call(
        kern,
        grid=(t // tm, n // tn),
        in_specs=[pl.BlockSpec((tm, d), lambda i, j: (i, 0)), vec, vec, vec,
                  pl.BlockSpec((d, tn), lambda i, j: (0, j)),
                  pl.BlockSpec((tm, HEAD_DIM), lambda i, j: (i, 0)),
                  pl.BlockSpec((tm, HEAD_DIM), lambda i, j: (i, 0))],
        out_specs=pl.BlockSpec((tm, tn), lambda i, j: (i, j)),
        out_shape=jax.ShapeDtypeStruct((t, n), BF16),
        scratch_shapes=[pltpu.VMEM((tm, d), BF16), pltpu.VMEM((tm, tn), F32)],
        compiler_params=_cparams(("arbitrary", "arbitrary")),
        name="inproj",
    )(x, g, sc, sh, w_bf16, cos, sin)


def _attn_kernel(q_ref, kc_ref, kp_ref, vc_ref, vp_ref, o_ref, lse_ref, kf, vf, *, n_heads, n_blocks):
    c = pl.program_id(1)
    w = Q_BLOCK
    kf[0:w, :] = kp_ref[...]
    kf[w:, :] = kc_ref[...]
    vf[0:w, :] = vp_ref[...]
    vf[w:, :] = vc_ref[...]
    qi = lax.broadcasted_iota(jnp.int32, (w, 2 * w), 0)
    kk = lax.broadcasted_iota(jnp.int32, (w, 2 * w), 1)
    band = jnp.logical_and(kk >= qi, kk <= qi + w)
    lane = lax.broadcasted_iota(jnp.int32, (w, LANES), 1)

    for h in range(n_heads):
        cols = slice(h * HEAD_DIM, (h + 1) * HEAD_DIM)

        def body(g, _, cols=cols, h=h):
            r0 = pl.multiple_of(g * w, w)
            q = q_ref[pl.ds(r0, w), cols]
            k = kf[pl.ds(r0, 2 * w), cols]
            v = vf[pl.ds(r0, 2 * w), cols]
            s = lax.dot_general(q, k, (((1,), (1,)), ((), ())), preferred_element_type=F32)
            pad = jnp.logical_and(c == 0, g == 0)
            valid = jnp.logical_and(band, jnp.logical_or(kk >= w, jnp.logical_not(pad)))
            s = jnp.where(valid, s, NEG_INF)
            m = jnp.max(s, axis=-1, keepdims=True)
            p = jnp.exp(s - m)
            den = jnp.sum(p, axis=-1, keepdims=True)
            o = jnp.dot(p.astype(BF16), v, preferred_element_type=F32) / den
            o_ref[pl.ds(r0, w), cols] = o.astype(o_ref.dtype)
            lse = m + jnp.log(den)
            prev = lse_ref[pl.ds(r0, w), :] if h > 0 else jnp.zeros((w, LANES), F32)
            lse_ref[pl.ds(r0, w), :] = jnp.where(lane == h, lse, prev)
            return 0

        lax.fori_loop(0, n_blocks, body, 0)


def _attn_branch(z, window, dilation, attn_width):
    t, n = z.shape
    d = dilation
    assert window // d == Q_BLOCK
    length = t // d
    nb = min(8, length // Q_BLOCK)
    rows = nb * Q_BLOCK
    zc = n // attn_width
    zv = z.reshape(length, d * n)
    kern = functools.partial(_attn_kernel, n_heads=attn_width // HEAD_DIM, n_blocks=nb)

    def cur(off):
        return pl.BlockSpec((rows, attn_width), lambda r, c: (c, zc * r + off))

    def prev(off):
        return pl.BlockSpec((Q_BLOCK, attn_width), lambda r, c: (jnp.maximum(c * nb - 1, 0), zc * r + off))

    o, lse = pl.pallas_call(
        kern,
        grid=(d, length // rows),
        in_specs=[cur(0), cur(1), prev(1), cur(2), prev(2)],
        out_specs=[pl.BlockSpec((rows, attn_width), lambda r, c: (c, r)),
                   pl.BlockSpec((rows, LANES), lambda r, c: (c, r))],
        out_shape=[jax.ShapeDtypeStruct((length, d * attn_width), BF16),
                   jax.ShapeDtypeStruct((length, d * LANES), F32)],
        scratch_shapes=[pltpu.VMEM((rows + Q_BLOCK, attn_width), BF16),
                        pltpu.VMEM((rows + Q_BLOCK, attn_width), BF16)],
        compiler_params=_cparams(("arbitrary", "arbitrary")),
        name=f"attn_d{d}",
    )(zv, zv, zv, zv, zv)
    return o.reshape(t, attn_width), lse.reshape(t, LANES)


def _attn_combine_kernel(o1, o2, o3, l1, l2, l3, g_ref, out_ref, acc_scr, *, n_heads):
    la, lb, lc = l1[...], l2[...], l3[...]
    m = jnp.maximum(jnp.maximum(la, lb), lc)
    ea, eb, ec = jnp.exp(la - m), jnp.exp(lb - m), jnp.exp(lc - m)
    inv = 1.0 / (ea + eb + ec)
    wa, wb, wc = ea * inv, eb * inv, ec * inv
    ss = jnp.zeros((la.shape[0], 1), F32)
    for h in range(n_heads):
        cols = slice(h * HEAD_DIM, (h + 1) * HEAD_DIM)
        y = (wa[:, h:h + 1] * o1[:, cols].astype(F32) + wb[:, h:h + 1] * o2[:, cols].astype(F32)
             + wc[:, h:h + 1] * o3[:, cols].astype(F32))
        acc_scr[:, cols] = y
        ss = ss + jnp.sum(y * y, axis=-1, keepdims=True)
    scale = lax.rsqrt(ss / (n_heads * HEAD_DIM) + NORM_EPS)
    out_ref[...] = (acc_scr[...] * scale * g_ref[...]).astype(BF16)


def _attn_combine(outs, lses, g):
    t, width = outs[0].shape
    tm = min(t, 256)
    ob = pl.BlockSpec((tm, width), lambda i: (i, 0))
    lb = pl.BlockSpec((tm, LANES), lambda i: (i, 0))
    return pl.pallas_call(
        functools.partial(_attn_combine_kernel, n_heads=width // HEAD_DIM),
        grid=(t // tm,),
        in_specs=[ob, ob, ob, lb, lb, lb, pl.BlockSpec((1, width), lambda i: (0, 0))],
        out_specs=ob,
        out_shape=jax.ShapeDtypeStruct((t, width), BF16),
        scratch_shapes=[pltpu.VMEM((tm, width), F32)],
        compiler_params=_cparams(("arbitrary",)),
        name="attn_combine",
    )(*outs, *lses, g)


def _softplus(x):
    return jnp.maximum(x, 0.0) + jnp.log1p(jnp.exp(-jnp.abs(x)))


def _gelu_tanh(x):
    return 0.5 * x * (1.0 + jnp.tanh(0.7978845608028654 * (x + 0.044715 * x * x * x)))


def _lru_kernel(xr_ref, gr_ref, pos_ref, cw_ref, cb_ref, wg_ref, ba_ref, bx_ref, lam_ref, g_ref,
                o_ref, xbuf, a_scr, b_scr, hcar, *, n_blocks):
    i = pl.program_id(0)
    tc, width = a_scr.shape
    pad = SUBLANES
    bw = LRU_BLOCK_W

    @pl.when(i == 0)
    def _():
        xbuf[0:pad, :] = jnp.zeros((pad, width), F32)
        hcar[...] = jnp.zeros(hcar.shape, F32)

    xbuf[pad:, :] = xr_ref[...].astype(F32)
    reset = pos_ref[...] == 0
    sub = lax.broadcasted_iota(jnp.int32, (tc, bw), 0) % SUBLANES

    for hb in range(n_blocks):
        cols = slice(hb * bw, (hb + 1) * bw)
        xc = cb_ref[:, cols] + cw_ref[0:1, cols] * xbuf[pad - 3:pad - 3 + tc, cols]
        for k in range(1, CONV_WIDTH):
            xc = xc + cw_ref[k:k + 1, cols] * xbuf[pad - 3 + k:pad - 3 + k + tc, cols]
        gates = jnp.dot(xc.astype(BF16), wg_ref[hb], preferred_element_type=F32)
        r = _sigmoid(gates[:, :bw] + ba_ref[:, cols])
        ig = _sigmoid(gates[:, bw:] + bx_ref[:, cols])
        log_a = -LRU_C * r * _softplus(-lam_ref[:, cols])
        ea = jnp.exp(log_a)
        a = jnp.where(reset, 0.0, ea)
        mult = jnp.where(reset, 1.0, jnp.sqrt(-jnp.tanh(log_a) * (ea * ea + 1.0)))
        b = xc * ig * mult
        for s in (1, 2, 4):
            a_s = pltpu.roll(a, s, axis=0)
            b_s = pltpu.roll(b, s, axis=0)
            keep = sub >= s
            b = jnp.where(keep, a * b_s + b, b)
            a = jnp.where(keep, a * a_s, a)
        a_scr[:, cols] = a
        b_scr[:, cols] = b

    def group(gi, h_in):
        r0 = pl.multiple_of(gi * SUBLANES, SUBLANES)
        hh = a_scr[pl.ds(r0, SUBLANES), :] * h_in + b_scr[pl.ds(r0, SUBLANES), :]
        a_scr[pl.ds(r0, SUBLANES), :] = hh
        return jnp.broadcast_to(hh[SUBLANES - 1:SUBLANES, :], hh.shape)

    hcar[...] = lax.fori_loop(0, tc // SUBLANES, group, hcar[...])
    xbuf[0:pad, :] = xbuf[tc:tc + pad, :]

    rows = 32

    def epilogue(si, _):
        r0 = pl.multiple_of(si * rows, rows)
        y = a_scr[pl.ds(r0, rows), :] * _gelu_tanh(gr_ref[pl.ds(r0, rows), :].astype(F32))
        ms = jnp.mean(y * y, axis=-1, keepdims=True)
        o_ref[pl.ds(r0, rows), :] = (y * lax.rsqrt(ms + NORM_EPS) * g_ref[...]).astype(BF16)
        return 0

    lax.fori_loop(0, tc // rows, epilogue, 0)


def _lru(z, pos_col, conv_w, conv_b, wgate_bf16, ba, bx, lam, g, col_block):
    t = z.shape[0]
    width = conv_w.shape[1]
    tc = min(t, 128)
    n_blocks = width // LRU_BLOCK_W
    vec = pl.BlockSpec((1, width), lambda i: (0, 0))
    return pl.pallas_call(
        functools.partial(_lru_kernel, n_blocks=n_blocks),
        grid=(t // tc,),
        in_specs=[pl.BlockSpec((tc, width), lambda i: (i, col_block)),
                  pl.BlockSpec((tc, width), lambda i: (i, col_block + 1)),
                  pl.BlockSpec((tc, 1), lambda i: (i, 0)),
                  pl.BlockSpec((CONV_WIDTH, width), lambda i: (0, 0)), vec,
                  pl.BlockSpec((n_blocks, LRU_BLOCK_W, 2 * LRU_BLOCK_W), lambda i: (0, 0, 0)),
                  vec, vec, vec, vec],
        out_specs=pl.BlockSpec((tc, width), lambda i: (i, 0)),
        out_shape=jax.ShapeDtypeStruct((t, width), BF16),
        scratch_shapes=[pltpu.VMEM((tc + SUBLANES, width), F32), pltpu.VMEM((tc, width), F32),
                        pltpu.VMEM((tc, width), F32), pltpu.VMEM((SUBLANES, width), F32)],
        compiler_params=_cparams(("arbitrary",)),
        name="rg_lru",
    )(z, z, pos_col, conv_w, conv_b, wgate_bf16, ba, bx, lam, g)


def _outproj_kernel(a_ref, r_ref, wa_ref, wr_ref, x_ref, g1_ref, n2_ref, sc_ref, sh_ref, rw_ref, rb_ref,
                    x1_ref, hp_ref, ti_ref, tg_ref, *, n_experts):
    y = jnp.dot(a_ref[...], wa_ref[...], preferred_element_type=F32)
    y = y + jnp.dot(r_ref[...], wr_ref[...], preferred_element_type=F32)
    x1 = x_ref[...] + g1_ref[...] * y
    x1_ref[...] = x1
    ms = jnp.mean(x1 * x1, axis=-1, keepdims=True)
    h = x1 * lax.rsqrt(ms + NORM_EPS) * (n2_ref[...] * (1.0 + sc_ref[...])) + sh_ref[...]
    half = h.shape[1] // 2
    bits = pltpu.bitcast(h.astype(BF16).astype(F32), U32)
    hp_ref[...] = jnp.bitwise_or(lax.shift_right_logical(bits[:, :half], jnp.uint32(16)),
                                 jnp.bitwise_and(bits[:, half:], jnp.uint32(0xFFFF0000)))
    logits = jnp.dot(h, rw_ref[...], preferred_element_type=F32, precision=lax.Precision.HIGHEST)
    logits = logits + rb_ref[...]
    lane = lax.broadcasted_iota(jnp.int32, logits.shape, 1).astype(F32)
    cur = jnp.where(lane < n_experts, logits, -jnp.inf)
    vals, idxs = [], []
    for _ in range(TOP_K):
        m = jnp.max(cur, axis=-1, keepdims=True)
        idx = jnp.min(jnp.where(cur == m, lane, float(LANES)), axis=-1, keepdims=True)
        vals.append(m)
        idxs.append(idx)
        cur = jnp.where(lane == idx, -jnp.inf, cur)
    es = [jnp.exp(v - vals[0]) for v in vals]
    inv = 1.0 / (es[0] + es[1] + es[2] + es[3])
    ti = jnp.zeros(logits.shape, F32)
    tg = jnp.zeros(logits.shape, F32)
    for k in range(TOP_K):
        ti = jnp.where(lane == k, idxs[k], ti)
        tg = jnp.where(lane == k, es[k] * inv, tg)
    ti_ref[...] = ti
    tg_ref[...] = tg


def _outproj(attn_n, lru_n, wa_bf16, wr_bf16, x, g1, n2, sc, sh, rw_pad, rb_pad, n_experts):
    t, d = x.shape
    wa = attn_n.shape[1]
    wr = lru_n.shape[1]
    tm = min(t, 256)
    vec = pl.BlockSpec((1, d), lambda i: (0, 0))
    lanes = pl.BlockSpec((tm, LANES), lambda i: (i, 0))
    return pl.pallas_call(
        functools.partial(_outproj_kernel, n_experts=n_experts),
        grid=(t // tm,),
        in_specs=[pl.BlockSpec((tm, wa), lambda i: (i, 0)), pl.BlockSpec((tm, wr), lambda i: (i, 0)),
                  pl.BlockSpec((wa, d), lambda i: (0, 0)), pl.BlockSpec((wr, d), lambda i: (0, 0)),
                  pl.BlockSpec((tm, d), lambda i: (i, 0)), vec, vec, vec, vec,
                  pl.BlockSpec((d, LANES), lambda i: (0, 0)), pl.BlockSpec((1, LANES), lambda i: (0, 0))],
        out_specs=[pl.BlockSpec((tm, d), lambda i: (i, 0)), pl.BlockSpec((tm, d // 2), lambda i: (i, 0)),
                   lanes, lanes],
        out_shape=[jax.ShapeDtypeStruct((t, d), F32), jax.ShapeDtypeStruct((t, d // 2), U32),
                   jax.ShapeDtypeStruct((t, LANES), F32), jax.ShapeDtypeStruct((t, LANES), F32)],
        compiler_params=_cparams(("arbitrary",)),
        name="outproj",
    )(attn_n, lru_n, wa_bf16, wr_bf16, x, g1, n2, sc, sh, rw_pad, rb_pad)


def _route_kernel(ti_ref, dest_ref, cnt_ref, carry, gstart):
    ph = pl.program_id(0)
    i = pl.program_id(1)
    last = pl.num_programs(1) - 1
    tm = ti_ref.shape[0]
    ti = ti_ref[...]
    lane = lax.broadcasted_iota(jnp.int32, (tm, LANES), 1).astype(F32)
    sel = [lane == ti[:, k:k + 1] for k in range(TOP_K)]
    hot = jnp.zeros((tm, LANES), F32)
    for k in range(TOP_K):
        hot = jnp.where(sel[k], 1.0, hot)

    @pl.when(jnp.logical_and(ph == 0, i == 0))
    def _():
        carry[...] = jnp.zeros(carry.shape, F32)

    @pl.when(ph == 1)
    def _():
        row = lax.broadcasted_iota(jnp.int32, (tm, tm), 0)
        col = lax.broadcasted_iota(jnp.int32, (tm, tm), 1)
        tri = jnp.where(col < row, 1.0, 0.0).astype(BF16)
        before = jnp.dot(tri, hot.astype(BF16), preferred_element_type=F32) + carry[0:1, :]
        base = before + gstart[0:1, :]
        dest = jnp.zeros((tm, LANES), F32)
        for k in range(TOP_K):
            dk = jnp.sum(jnp.where(sel[k], base, 0.0), axis=-1, keepdims=True)
            dest = jnp.where(lane == k, dk, dest)
        dest_ref[...] = dest.astype(jnp.int32)

    carry[...] = carry[...] + jnp.sum(hot, axis=0, keepdims=True)

    @pl.when(jnp.logical_and(ph == 0, i == last))
    def _():
        cnt = carry[...]
        cnt_ref[...] = cnt[0:1, :]
        padded = jnp.ceil(cnt / MOE_UNIT) * MOE_UNIT
        lane8 = lax.broadcasted_iota(jnp.int32, cnt.shape, 1)
        acc = padded
        for s in (1, 2, 4, 8, 16, 32, 64):
            acc = acc + jnp.where(lane8 >= s, pltpu.roll(acc, s, axis=1), 0.0)
        gstart[...] = acc - padded
        carry[...] = jnp.zeros(carry.shape, F32)


def _route(topi):
    t = topi.shape[0]
    tm = min(t, TOK_TILE)
    dest, cnt = pl.pallas_call(
        _route_kernel,
        grid=(2, t // tm),
        in_specs=[pl.BlockSpec((tm, LANES), lambda p, i: (i, 0))],
        out_specs=[pl.BlockSpec((tm, LANES), lambda p, i: (i * p, 0)),
                   pl.BlockSpec((1, LANES), lambda p, i: (0, 0))],
        out_shape=[jax.ShapeDtypeStruct((t, LANES), jnp.int32), jax.ShapeDtypeStruct((1, LANES), F32)],
        scratch_shapes=[pltpu.VMEM((SUBLANES, LANES), F32), pltpu.VMEM((SUBLANES, LANES), F32)],
        compiler_params=_cparams(("arbitrary", "arbitrary")),
        name="route",
    )(topi)
    return dest, cnt


def _row_copy(src, dst, sem):
    return pltpu.make_async_copy(src, dst, sem)


def _dispatch_kernel(dest_sm, h_ref, xs_in, xs_ref, sem):
    del xs_in
    i = pl.program_id(0)
    tm = h_ref.shape[0]

    def issue(r, _):
        for k in range(TOP_K):
            d = dest_sm[(i * tm + r) * TOP_K + k]
            _row_copy(h_ref.at[pl.ds(r, 1), :], xs_ref.at[pl.ds(d, 1), :], sem).start()
        return 0

    lax.fori_loop(0, tm, issue, 0)
    for _ in range(TOP_K):
        _row_copy(h_ref, xs_ref.at[pl.ds(0, tm), :], sem).wait()


def _dispatch(dest_flat, hp, n_rows):
    t, dw = hp.shape
    tm = min(t, TOK_TILE)
    xs0 = jnp.zeros((n_rows, dw), U32)
    return pl.pallas_call(
        _dispatch_kernel,
        grid_spec=pltpu.PrefetchScalarGridSpec(
            num_scalar_prefetch=1,
            grid=(t // tm,),
            in_specs=[pl.BlockSpec((tm, dw), lambda i, d: (i, 0)),
                      pl.BlockSpec(memory_space=pl.ANY)],
            out_specs=pl.BlockSpec(memory_space=pl.ANY),
            scratch_shapes=[pltpu.SemaphoreType.DMA],
        ),
        out_shape=jax.ShapeDtypeStruct((n_rows, dw), U32),
        input_output_aliases={2: 0},
        compiler_params=_cparams(("arbitrary",)),
        name="dispatch",
    )(dest_flat, hp, xs0)


def _moe_kernel(ue_sm, ur_sm, na_sm, x_ref, w1_ref, b1_ref, w2_ref, b2_ref, sel_ref, o_ref, w1b, w2b):
    u = pl.program_id(0)
    f = pl.program_id(1)
    half = x_ref.shape[1]
    tf2 = w1_ref.shape[1]

    @pl.when(u < na_sm[0])
    def _():
        w1b[...] = w1_ref[...].astype(BF16)
        w2b[...] = w2_ref[...].astype(BF16)
        n_sub = (ur_sm[u] + MOE_SUB - 1) // MOE_SUB

        def sub_block(bi, _):
            r0 = pl.multiple_of(bi * MOE_SUB, MOE_SUB)
            xi = x_ref[pl.ds(r0, MOE_SUB), :]
            lo = pltpu.bitcast(lax.shift_left(xi, jnp.uint32(16)), F32).astype(BF16)
            hi = pltpu.bitcast(jnp.bitwise_and(xi, jnp.uint32(0xFFFF0000)), F32).astype(BF16)
            uu = jnp.dot(lo, w1b[0:half, :], preferred_element_type=F32)
            uu = uu + jnp.dot(hi, w1b[half:, :], preferred_element_type=F32) + b1_ref[...]
            glu = jnp.minimum(uu, SWIGLU_LIMIT)
            glu = glu * _sigmoid(SWIGLU_ALPHA * glu)
            lin = jnp.clip(uu, -SWIGLU_LIMIT, SWIGLU_LIMIT) + 1.0
            acts = []
            for cidx in range(tf2 // (2 * LANES)):
                cs = slice(cidx * 2 * LANES, (cidx + 1) * 2 * LANES)
                prod = glu[:, cs] * pltpu.roll(lin[:, cs], 2 * LANES - 1, axis=1)
                acts.append(jnp.dot(prod.astype(BF16), sel_ref[...], preferred_element_type=F32))
            act = jnp.concatenate(acts, axis=1).astype(BF16)
            y = jnp.dot(act, w2b[...], preferred_element_type=F32)

            @pl.when(f == 0)
            def _():
                o_ref[pl.ds(r0, MOE_SUB), :] = y + b2_ref[...]

            @pl.when(f > 0)
            def _():
                o_ref[pl.ds(r0, MOE_SUB), :] = o_ref[pl.ds(r0, MOE_SUB), :] + y

            return 0

        lax.fori_loop(0, n_sub, sub_block, 0)


def _moe(xs, w1, b1, w2, b2, layer, unit_expert, unit_rows, n_active):
    n_rows, half = xs.shape
    n_layers, n_exp, d, ff2 = w1.shape
    ff = ff2 // 2
    n_units = n_rows // MOE_UNIT
    nf = ff // MOE_TF
    sel = (jnp.arange(2 * LANES)[:, None] == 2 * jnp.arange(LANES)[None, :]).astype(BF16)

    def unit(u, na):
        return jnp.minimum(u, na[0] - 1)

    def fstep(u, f, na):
        return jnp.where(u < na[0], f, nf - 1)

    return pl.pallas_call(
        _moe_kernel,
        grid_spec=pltpu.PrefetchScalarGridSpec(
            num_scalar_prefetch=3,
            grid=(n_units, nf),
            in_specs=[
                pl.BlockSpec((MOE_UNIT, half), lambda u, f, ue, ur, na: (unit(u, na), 0)),
                pl.BlockSpec((None, None, d, 2 * MOE_TF),
                             lambda u, f, ue, ur, na: (layer, ue[u], 0, fstep(u, f, na))),
                pl.BlockSpec((None, None, 1, 2 * MOE_TF),
                             lambda u, f, ue, ur, na: (layer, ue[u], 0, fstep(u, f, na))),
                pl.BlockSpec((None, None, MOE_TF, d),
                             lambda u, f, ue, ur, na: (layer, ue[u], fstep(u, f, na), 0)),
                pl.BlockSpec((None, None, 1, d), lambda u, f, ue, ur, na: (layer, ue[u], 0, 0)),
                pl.BlockSpec((2 * LANES, LANES), lambda u, f, ue, ur, na: (0, 0)),
            ],
            out_specs=pl.BlockSpec((MOE_UNIT, d), lambda u, f, ue, ur, na: (unit(u, na), 0)),
            scratch_shapes=[pltpu.VMEM((d, 2 * MOE_TF), BF16), pltpu.VMEM((MOE_TF, d), BF16)],
        ),
        out_shape=jax.ShapeDtypeStruct((n_rows, d), F32),
        compiler_params=_cparams(("arbitrary", "arbitrary")),
        name="moe_experts",
    )(unit_expert, unit_rows, n_active, xs, w1, b1.reshape(n_layers, n_exp, 1, ff2), w2,
      b2.reshape(n_layers, n_exp, 1, d), sel)


def _combine_kernel(dest_sm, ys_ref, tg_ref, x1_ref, g2_ref, fg_ref, o_ref, buf, sem, *, final_norm):
    i = pl.program_id(0)
    tm = x1_ref.shape[0]

    def issue(r, _):
        for k in range(TOP_K):
            d = dest_sm[(i * tm + r) * TOP_K + k]
            _row_copy(ys_ref.at[pl.ds(d, 1), :], buf.at[k, pl.ds(r, 1), :], sem).start()
        return 0

    lax.fori_loop(0, tm, issue, 0)
    for k in range(TOP_K):
        _row_copy(ys_ref.at[pl.ds(0, tm), :], buf.at[k], sem).wait()

    rows = 32

    def body(si, _):
        r0 = pl.multiple_of(si * rows, rows)
        tg = tg_ref[pl.ds(r0, rows), :]
        y = tg[:, 0:1] * buf[0, pl.ds(r0, rows), :]
        for k in range(1, TOP_K):
            y = y + tg[:, k:k + 1] * buf[k, pl.ds(r0, rows), :]
        x2 = x1_ref[pl.ds(r0, rows), :] + g2_ref[...] * y
        if final_norm:
            ms = jnp.mean(x2 * x2, axis=-1, keepdims=True)
            x2 = x2 * lax.rsqrt(ms + NORM_EPS) * fg_ref[...]
        o_ref[pl.ds(r0, rows), :] = x2
        return 0

    lax.fori_loop(0, tm // rows, body, 0)


def _combine(dest_flat, ys, tgate, x1, g2, final_g, final_norm):
    t, d = x1.shape
    tm = min(t, TOK_TILE)
    vec = pl.BlockSpec((1, d), lambda i, dd: (0, 0))
    return pl.pallas_call(
        functools.partial(_combine_kernel, final_norm=final_norm),
        grid_spec=pltpu.PrefetchScalarGridSpec(
            num_scalar_prefetch=1,
            grid=(t // tm,),
            in_specs=[pl.BlockSpec(memory_space=pl.ANY),
                      pl.BlockSpec((tm, LANES), lambda i, dd: (i, 0)),
                      pl.BlockSpec((tm, d), lambda i, dd: (i, 0)), vec, vec],
            out_specs=pl.BlockSpec((tm, d), lambda i, dd: (i, 0)),
            scratch_shapes=[pltpu.VMEM((TOP_K, tm, d), F32), pltpu.SemaphoreType.DMA],
        ),
        out_shape=jax.ShapeDtypeStruct((t, d), F32),
        compiler_params=_cparams(("arbitrary",)),
        name="moe_combine",
    )(dest_flat, ys, tgate, x1, g2, final_g)


def _unit_tables(counts, n_units):
    units_per = (counts + MOE_UNIT - 1) // MOE_UNIT
    ends = jnp.cumsum(units_per)
    starts = ends - units_per
    n_active = ends[-1]
    u = jnp.arange(n_units, dtype=jnp.int32)
    ue = jnp.minimum(jnp.searchsorted(ends, jnp.minimum(u, n_active - 1), side="right"),
                     counts.shape[0] - 1).astype(jnp.int32)
    rows = jnp.clip(counts[ue] - (u - starts[ue]) * MOE_UNIT, 0, MOE_UNIT)
    rows = jnp.where(u < n_active, rows, 0).astype(jnp.int32)
    return ue, rows, n_active.reshape(1).astype(jnp.int32)


def kernel(x, c, positions, ada_w, ada_b, norm1_g, norm2_g, w_in, conv_w, conv_b, lru_wa, lru_ba, lru_wx,
           lru_bx, lru_lambda, attn_out_g, lru_out_g, w_out, router_w, router_b, w1, b1, w2, b2, final_g):
    bsz, seq, d = x.shape
    assert bsz == 1
    t = seq
    n_layers = ada_w.shape[0]
    attn_width = attn_out_g.shape[1]
    lru_width = lru_out_g.shape[1]
    n_experts = router_w.shape[2]
    assert attn_width == lru_width and w_in.shape[2] == 3 * attn_width + 2 * lru_width
    n_units = (t * TOP_K) // MOE_UNIT + n_experts
    n_rows = n_units * MOE_UNIT

    xf = x.reshape(t, d)
    pos_col = positions.reshape(t, 1)
    mod = _adaln(c, ada_w, ada_b)
    cos, sin = _rope_tables(pos_col)
    fg = final_g.reshape(1, d)

    for l in range(n_layers):
        sh1, sc1, g1, sh2, sc2, g2 = [mod[l, :, i * d:(i + 1) * d] for i in range(6)]
        z = _inproj(xf, norm1_g[l].reshape(1, d), sc1, sh1, w_in[l].astype(BF16), cos, sin, attn_width)
        outs, lses = [], []
        for window, dilation in DILATED_BRANCHES:
            o, lse = _attn_branch(z, window, dilation, attn_width)
            outs.append(o)
            lses.append(lse)
        attn_n = _attn_combine(outs, lses, attn_out_g[l].reshape(1, attn_width))
        wgate = jnp.concatenate([lru_wa[l], lru_wx[l]], axis=-1).astype(BF16)
        lru_n = _lru(z, pos_col, conv_w[l], conv_b[l].reshape(1, -1), wgate, lru_ba[l].reshape(1, -1),
                     lru_bx[l].reshape(1, -1), lru_lambda[l].reshape(1, -1), lru_out_g[l].reshape(1, -1),
                     3 * attn_width // lru_width)
        wo = w_out[l].astype(BF16)
        rw_pad = jnp.pad(router_w[l], ((0, 0), (0, LANES - n_experts)))
        rb_pad = jnp.pad(router_b[l], (0, LANES - n_experts)).reshape(1, LANES)
        x1, hp, topi, tgate = _outproj(attn_n, lru_n, wo[:attn_width], wo[attn_width:], xf, g1,
                                       norm2_g[l].reshape(1, d), sc2, sh2, rw_pad, rb_pad, n_experts)
        dest, cnt = _route(topi)
        counts = cnt[0, :n_experts].astype(jnp.int32)
        unit_expert, unit_rows, n_active = _unit_tables(counts, n_units)
        dest_flat = dest[:, :TOP_K].reshape(t * TOP_K)
        xs = _dispatch(dest_flat, hp, n_rows)
        ys = _moe(xs, w1, b1, w2, b2, l, unit_expert, unit_rows, n_active)
        xf = _combine(dest_flat, ys, tgate, x1, g2, fg, l == n_layers - 1)
    return xf.reshape(bsz, seq, d)
```

```python
import functools

import jax
import jax.numpy as jnp
from jax import lax
from jax.experimental import pallas as pl
from jax.experimental.pallas import tpu as pltpu

F32 = jnp.float32
BF16 = jnp.bfloat16

HEAD_DIM = 128
LRU_BLOCK_W = 128
CONV_WIDTH = 4
LRU_C = 8.0
ROPE_THETA = 10000.0
DILATED_BRANCHES = ((128, 1), (512, 4), (2048, 16))
Q_BLOCK = 128
NEG_INF = -1e30
NORM_EPS = 1e-6
TOP_K = 4
SWIGLU_LIMIT = 7.0
SWIGLU_ALPHA = 1.702

LANES = 128
SUBLANES = 8
VMEM_LIMIT = 56 * 1024 * 1024

MOE_SUB = 256
MOE_UNIT = 1536
MOE_TF = 512
TOK_TILE = 256


def _cparams(sem, vmem=VMEM_LIMIT):
    return pltpu.CompilerParams(dimension_semantics=sem, vmem_limit_bytes=vmem)


def _sigmoid(x):
    return 0.5 * (1.0 + jnp.tanh(0.5 * x))


def _adaln_kernel(c_ref, w_ref, b_ref, o_ref, cond):
    d, tn = w_ref.shape
    groups = 4
    rows = groups * SUBLANES
    tiles = tn // LANES

    @pl.when(jnp.logical_and(pl.program_id(0) == 0, pl.program_id(1) == 0))
    def _():
        cv = c_ref[...]
        cond[...] = jnp.broadcast_to(cv * _sigmoid(cv), cond.shape)

    def body(i, accs):
        r = pl.multiple_of(i * rows, rows)
        out = []
        for g in range(groups):
            rg = r + g * SUBLANES
            cv = cond[pl.ds(rg, SUBLANES), :]
            for ti in range(tiles):
                out.append(accs[g * tiles + ti] + w_ref[pl.ds(rg, SUBLANES), ti * LANES:(ti + 1) * LANES] * cv)
        return tuple(out)

    zero = jnp.zeros((SUBLANES, LANES), F32)
    accs = lax.fori_loop(0, d // rows, body, tuple(zero for _ in range(groups * tiles)))
    for ti in range(tiles):
        acc = (accs[ti] + accs[tiles + ti]) + (accs[2 * tiles + ti] + accs[3 * tiles + ti])
        cols = slice(ti * LANES, (ti + 1) * LANES)
        o_ref[:, cols] = jnp.sum(acc, axis=0, keepdims=True) + b_ref[:, cols]


def _adaln(c, ada_w, ada_b):
    n_layers, d, n = ada_w.shape
    tn = 1024
    return pl.pallas_call(
        _adaln_kernel,
        grid=(n_layers, n // tn),
        in_specs=[pl.BlockSpec((d, 1), lambda l, j: (0, 0)),
                  pl.BlockSpec((None, d, tn), lambda l, j: (l, 0, j)),
                  pl.BlockSpec((None, 1, tn), lambda l, j: (l, 0, j))],
        out_specs=pl.BlockSpec((None, 1, tn), lambda l, j: (l, 0, j)),
        out_shape=jax.ShapeDtypeStruct((n_layers, 1, n), F32),
        scratch_shapes=[pltpu.VMEM((d, LANES), F32)],
        compiler_params=_cparams(("arbitrary", "arbitrary")),
        name="adaln",
    )(c.reshape(d, 1), ada_w, ada_b.reshape(n_layers, 1, n))


def _rope_kernel(pos_ref, invf_ref, cos_ref, sin_ref):
    ang = pos_ref[...].astype(F32) * invf_ref[...]
    cos_ref[...] = jnp.cos(ang)
    s = jnp.sin(ang)
    lane = lax.broadcasted_iota(jnp.int32, s.shape, 1)
    sin_ref[...] = jnp.where(lane < HEAD_DIM // 2, -s, s)


def _rope_tables(pos_col):
    t = pos_col.shape[0]
    tm = min(t, 1024)
    inv = ROPE_THETA ** (-jnp.arange(0, HEAD_DIM, 2, dtype=F32) / HEAD_DIM)
    inv = jnp.concatenate([inv, inv]).reshape(1, HEAD_DIM)
    return pl.pallas_call(
        _rope_kernel,
        grid=(t // tm,),
        in_specs=[pl.BlockSpec((tm, 1), lambda i: (i, 0)),
                  pl.BlockSpec((1, HEAD_DIM), lambda i: (0, 0))],
        out_specs=[pl.BlockSpec((tm, HEAD_DIM), lambda i: (i, 0))] * 2,
        out_shape=[jax.ShapeDtypeStruct((t, HEAD_DIM), F32)] * 2,
        compiler_params=_cparams(("arbitrary",)),
        name="rope_tables",
    )(pos_col, inv)


def _inproj_kernel(x_ref, g_ref, sc_ref, sh_ref, w_ref, cos_ref, sin_ref, o_ref, h_scr,
                   *, q_tiles, rope_tiles):
    j = pl.program_id(1)
    tm, tn = o_ref.shape
    rows = 32

    @pl.when(j == 0)
    def _():
        a = g_ref[...] * (1.0 + sc_ref[...])
        b = sh_ref[...]

        def body(i, _):
            r = pl.multiple_of(i * rows, rows)
            xv = x_ref[pl.ds(r, rows), :]
            ms = jnp.mean(xv * xv, axis=-1, keepdims=True)
            h_scr[pl.ds(r, rows), :] = (xv * lax.rsqrt(ms + NORM_EPS) * a + b).astype(BF16)
            return 0

        lax.fori_loop(0, tm // rows, body, 0)

    @pl.when(j < rope_tiles)
    def _():
        acc = jnp.dot(h_scr[...], w_ref[...], preferred_element_type=F32)
        scale = jnp.where(j < q_tiles, HEAD_DIM ** -0.5, 1.0).astype(F32)
        cs = cos_ref[...] * scale
        sn = sin_ref[...] * scale
        for c in range(tn // HEAD_DIM):
            cols = slice(c * HEAD_DIM, (c + 1) * HEAD_DIM)
            t = acc[:, cols]
            o_ref[:, cols] = (t * cs + pltpu.roll(t, HEAD_DIM // 2, axis=1) * sn).astype(BF16)

    @pl.when(j >= rope_tiles)
    def _():
        o_ref[...] = jnp.dot(h_scr[...], w_ref[...], preferred_element_type=F32).astype(BF16)


def _inproj(x, g, sc, sh, w_bf16, cos, sin, attn_width):
    t, d = x.shape
    n = w_bf16.shape[1]
    tm, tn = min(t, 1024), 512
    kern = functools.partial(_inproj_kernel, q_tiles=attn_width // tn, rope_tiles=2 * attn_width // tn)
    vec = pl.BlockSpec((1, d), lambda i, j: (0, 0))
    return pl.pallas_call(
        kern,
        grid=(t // tm, n // tn),
        in_specs=[pl.BlockSpec((tm, d), lambda i, j: (i, 0)), vec, vec, vec,
                  pl.BlockSpec((d, tn), lambda i, j: (0, j)),
                  pl.BlockSpec((tm, HEAD_DIM), lambda i, j: (i, 0)),
                  pl.BlockSpec((tm, HEAD_DIM), lambda i, j: (i, 0))],
        out_specs=pl.BlockSpec((tm, tn), lambda i, j: (i, j)),
        out_shape=jax.ShapeDtypeStruct((t, n), BF16),
        scratch_shapes=[pltpu.VMEM((tm, d), BF16)],
        compiler_params=_cparams(("arbitrary", "arbitrary")),
        name="inproj",
    )(x, g, sc, sh, w_bf16, cos, sin)


ATT_TILE = 2048
ATT_HEADS = 2
ATT_UNROLL = 4


def _rows(start, n, stride):
    return pl.ds(start, n) if stride == 1 else pl.ds(start, n, stride=stride)


def _attn_kernel(q_ref, kc_ref, kp_ref, vc_ref, vp_ref, o_ref, qf, kf, vf, oacc, lacc, *, dilations):
    m = pl.program_id(1)
    tp, wcols = q_ref.shape
    nh = wcols // HEAD_DIM
    w = Q_BLOCK
    n_blk = tp // w

    chunk = 256

    def widen(i, _):
        r = pl.multiple_of(i * chunk, chunk)
        for h in range(nh):
            cols = slice(h * HEAD_DIM, (h + 1) * HEAD_DIM)
            qf[h, pl.ds(r, chunk), :] = q_ref[pl.ds(r, chunk), cols].astype(F32)
            kf[h, pl.ds(r, chunk), :] = kp_ref[pl.ds(r, chunk), cols].astype(F32)
            kf[h, pl.ds(tp + r, chunk), :] = kc_ref[pl.ds(r, chunk), cols].astype(F32)
            vf[h, pl.ds(r, chunk), :] = vp_ref[pl.ds(r, chunk), cols].astype(F32)
            vf[h, pl.ds(tp + r, chunk), :] = vc_ref[pl.ds(r, chunk), cols].astype(F32)
        return 0

    lax.fori_loop(0, tp // chunk, widen, 0)

    qi = lax.broadcasted_iota(jnp.int32, (w, 2 * w), 0)
    kk = lax.broadcasted_iota(jnp.int32, (w, 2 * w), 1)
    band = jnp.logical_and(kk >= qi, kk <= qi + w)
    behind = kk >= w

    def block(bi, d, start, padded):
        start_k = start + tp - w * d
        for h in range(nh):
            q = qf[h, _rows(start, w, d), :].astype(BF16)
            k = kf[h, _rows(start_k, 2 * w, d), :].astype(BF16)
            v = vf[h, _rows(start_k, 2 * w, d), :].astype(BF16)
            s = lax.dot_general(q, k, (((1,), (1,)), ((), ())), preferred_element_type=F32)
            valid = jnp.logical_and(band, jnp.logical_or(behind, jnp.logical_not(padded)))
            s = jnp.where(valid, s, NEG_INF)
            mx = jnp.max(s, axis=-1, keepdims=True)
            p = jnp.exp(s - mx)
            den = jnp.sum(p, axis=-1, keepdims=True)
            o = jnp.dot(p.astype(BF16), v, preferred_element_type=F32) / den
            oacc[bi * nh + h, _rows(start, w, d), :] = o
            lacc[bi * nh + h, _rows(start, w, d), :] = jnp.broadcast_to(mx + jnp.log(den), (w, LANES))

    for bi, d in enumerate(dilations):
        per_class = n_blk // d

        def group(it, _, bi=bi, d=d, per_class=per_class):
            for j in range(ATT_UNROLL):
                idx = it * ATT_UNROLL + j
                r = idx // per_class
                nq = idx % per_class
                start = nq * (d * w) + r
                if d == 1:
                    start = pl.multiple_of(start, w)
                block(bi, d, start, jnp.logical_and(m == 0, nq == 0))
            return 0

        lax.fori_loop(0, n_blk // ATT_UNROLL, group, 0)

    rows = 64

    def mix(i, _):
        r = pl.multiple_of(i * rows, rows)
        for h in range(nh):
            cols = slice(h * HEAD_DIM, (h + 1) * HEAD_DIM)
            ls = [lacc[bi * nh + h, pl.ds(r, rows), :] for bi in range(len(dilations))]
            mx = functools.reduce(jnp.maximum, ls)
            es = [jnp.exp(l - mx) for l in ls]
            inv = 1.0 / functools.reduce(jnp.add, es)
            out = es[0] * inv * oacc[h, pl.ds(r, rows), :]
            for bi in range(1, len(dilations)):
                out = out + es[bi] * inv * oacc[bi * nh + h, pl.ds(r, rows), :]
            o_ref[pl.ds(r, rows), cols] = out.astype(o_ref.dtype)
        return 0

    lax.fori_loop(0, tp // rows, mix, 0)


def _attention(z, attn_width):
    t = z.shape[0]
    tp = ATT_TILE
    dilations = tuple(d for _, d in DILATED_BRANCHES)
    assert all(win // d == Q_BLOCK and win <= tp for win, d in DILATED_BRANCHES) and t % tp == 0
    wcols = ATT_HEADS * HEAD_DIM
    ng = attn_width // wcols

    def cur(off):
        return pl.BlockSpec((tp, wcols), lambda g, m: (m, off * ng + g))

    def prev(off):
        return pl.BlockSpec((tp, wcols), lambda g, m: (jnp.maximum(m - 1, 0), off * ng + g))

    nb = len(dilations)
    return pl.pallas_call(
        functools.partial(_attn_kernel, dilations=dilations),
        grid=(ng, t // tp),
        in_specs=[cur(0), cur(1), prev(1), cur(2), prev(2)],
        out_specs=pl.BlockSpec((tp, wcols), lambda g, m: (m, g)),
        out_shape=jax.ShapeDtypeStruct((t, attn_width), BF16),
        scratch_shapes=[pltpu.VMEM((ATT_HEADS, tp, HEAD_DIM), F32), pltpu.VMEM((ATT_HEADS, 2 * tp, HEAD_DIM), F32),
                        pltpu.VMEM((ATT_HEADS, 2 * tp, HEAD_DIM), F32),
                        pltpu.VMEM((nb * ATT_HEADS, tp, HEAD_DIM), F32),
                        pltpu.VMEM((nb * ATT_HEADS, tp, LANES), F32)],
        compiler_params=_cparams(("arbitrary", "arbitrary")),
        name="attention",
    )(z, z, z, z, z)


def _softplus(x):
    return jnp.maximum(x, 0.0) + jnp.log1p(jnp.exp(-jnp.abs(x)))


def _gelu_tanh(x):
    return 0.5 * x * (1.0 + jnp.tanh(0.7978845608028654 * (x + 0.044715 * x * x * x)))


def _lru_kernel(xr_ref, gr_ref, pos_ref, cw_ref, cb_ref, wg_ref, ba_ref, bx_ref, lam_ref, g_ref,
                o_ref, xbuf, a_scr, b_scr, hcar, *, n_blocks):
    i = pl.program_id(0)
    tc, width = a_scr.shape
    pad = SUBLANES
    bw = LRU_BLOCK_W

    @pl.when(i == 0)
    def _():
        xbuf[0:pad, :] = jnp.zeros((pad, width), F32)
        hcar[...] = jnp.zeros(hcar.shape, F32)

    xbuf[pad:, :] = xr_ref[...].astype(F32)
    reset = pos_ref[...] == 0
    sub = lax.broadcasted_iota(jnp.int32, (tc, bw), 0) % SUBLANES

    for hb in range(n_blocks):
        cols = slice(hb * bw, (hb + 1) * bw)
        xc = cb_ref[:, cols] + cw_ref[0:1, cols] * xbuf[pad - 3:pad - 3 + tc, cols]
        for k in range(1, CONV_WIDTH):
            xc = xc + cw_ref[k:k + 1, cols] * xbuf[pad - 3 + k:pad - 3 + k + tc, cols]
        gates = jnp.dot(xc.astype(BF16), wg_ref[hb], preferred_element_type=F32)
        r = _sigmoid(gates[:, :bw] + ba_ref[:, cols])
        ig = _sigmoid(gates[:, bw:] + bx_ref[:, cols])
        log_a = -LRU_C * r * _softplus(-lam_ref[:, cols])
        ea = jnp.exp(log_a)
        a = jnp.where(reset, 0.0, ea)
        mult = jnp.where(reset, 1.0, jnp.sqrt(-jnp.tanh(log_a) * (ea * ea + 1.0)))
        b = xc * ig * mult
        for s in (1, 2, 4):
            a_s = pltpu.roll(a, s, axis=0)
            b_s = pltpu.roll(b, s, axis=0)
            keep = sub >= s
            b = jnp.where(keep, a * b_s + b, b)
            a = jnp.where(keep, a * a_s, a)
        a_scr[:, cols] = a
        b_scr[:, cols] = b

    def group(gi, h_in):
        r0 = pl.multiple_of(gi * SUBLANES, SUBLANES)
        hh = a_scr[pl.ds(r0, SUBLANES), :] * h_in + b_scr[pl.ds(r0, SUBLANES), :]
        a_scr[pl.ds(r0, SUBLANES), :] = hh
        return jnp.broadcast_to(hh[SUBLANES - 1:SUBLANES, :], hh.shape)

    hcar[...] = lax.fori_loop(0, tc // SUBLANES, group, hcar[...])
    xbuf[0:pad, :] = xbuf[tc:tc + pad, :]

    rows = 32

    def epilogue(si, _):
        r0 = pl.multiple_of(si * rows, rows)
        y = a_scr[pl.ds(r0, rows), :] * _gelu_tanh(gr_ref[pl.ds(r0, rows), :].astype(F32))
        ms = jnp.mean(y * y, axis=-1, keepdims=True)
        o_ref[pl.ds(r0, rows), :] = (y * lax.rsqrt(ms + NORM_EPS) * g_ref[...]).astype(BF16)
        return 0

    lax.fori_loop(0, tc // rows, epilogue, 0)


def _lru(z, pos_col, conv_w, conv_b, wgate_bf16, ba, bx, lam, g, col_block):
    t = z.shape[0]
    width = conv_w.shape[1]
    tc = min(t, 128)
    n_blocks = width // LRU_BLOCK_W
    vec = pl.BlockSpec((1, width), lambda i: (0, 0))
    return pl.pallas_call(
        functools.partial(_lru_kernel, n_blocks=n_blocks),
        grid=(t // tc,),
        in_specs=[pl.BlockSpec((tc, width), lambda i: (i, col_block)),
                  pl.BlockSpec((tc, width), lambda i: (i, col_block + 1)),
                  pl.BlockSpec((tc, 1), lambda i: (i, 0)),
                  pl.BlockSpec((CONV_WIDTH, width), lambda i: (0, 0)), vec,
                  pl.BlockSpec((n_blocks, LRU_BLOCK_W, 2 * LRU_BLOCK_W), lambda i: (0, 0, 0)),
                  vec, vec, vec, vec],
        out_specs=pl.BlockSpec((tc, width), lambda i: (i, 0)),
        out_shape=jax.ShapeDtypeStruct((t, width), BF16),
        scratch_shapes=[pltpu.VMEM((tc + SUBLANES, width), F32), pltpu.VMEM((tc, width), F32),
                        pltpu.VMEM((tc, width), F32), pltpu.VMEM((SUBLANES, width), F32)],
        compiler_params=_cparams(("arbitrary",)),
        name="rg_lru",
    )(z, z, pos_col, conv_w, conv_b, wgate_bf16, ba, bx, lam, g)


def _outproj_kernel(a_ref, ag_ref, r_ref, wa_ref, wr_ref, x_ref, g1_ref, n2_ref, sc_ref, sh_ref, rw_ref, rb_ref,
                    x1_ref, h_ref, ti_ref, tg_ref, *, n_experts):
    a = a_ref[...].astype(F32)
    a = a * lax.rsqrt(jnp.mean(a * a, axis=-1, keepdims=True) + NORM_EPS) * ag_ref[...]
    y = jnp.dot(a.astype(BF16), wa_ref[...], preferred_element_type=F32)
    y = y + jnp.dot(r_ref[...], wr_ref[...], preferred_element_type=F32)
    x1 = x_ref[...] + g1_ref[...] * y
    x1_ref[...] = x1
    ms = jnp.mean(x1 * x1, axis=-1, keepdims=True)
    h = x1 * lax.rsqrt(ms + NORM_EPS) * (n2_ref[...] * (1.0 + sc_ref[...])) + sh_ref[...]
    h_ref[...] = h
    h_hi = h.astype(BF16)
    h_lo = (h - h_hi.astype(F32)).astype(BF16)
    p1 = jnp.dot(h_hi, rw_ref[...], preferred_element_type=F32)
    p2 = jnp.dot(h_lo, rw_ref[:, 0:LANES], preferred_element_type=F32)
    logits = p1[:, 0:LANES] + p1[:, LANES:] + p2 + rb_ref[...]
    lane = lax.broadcasted_iota(jnp.int32, logits.shape, 1).astype(F32)
    cur = jnp.where(lane < n_experts, logits, -jnp.inf)
    vals, idxs = [], []
    for _ in range(TOP_K):
        m = jnp.max(cur, axis=-1, keepdims=True)
        idx = jnp.min(jnp.where(cur == m, lane, float(LANES)), axis=-1, keepdims=True)
        vals.append(m)
        idxs.append(idx)
        cur = jnp.where(lane == idx, -jnp.inf, cur)
    es = [jnp.exp(v - vals[0]) for v in vals]
    inv = 1.0 / (es[0] + es[1] + es[2] + es[3])
    ti = jnp.zeros(logits.shape, F32)
    tg = jnp.zeros(logits.shape, F32)
    for k in range(TOP_K):
        ti = jnp.where(lane == k, idxs[k], ti)
        tg = jnp.where(lane == k, es[k] * inv, tg)
    ti_ref[...] = ti
    tg_ref[...] = tg


def _outproj(attn, attn_g, lru_n, wa_bf16, wr_bf16, x, g1, n2, sc, sh, rw_split, rb_pad, n_experts):
    t, d = x.shape
    wa = attn.shape[1]
    wr = lru_n.shape[1]
    tm = min(t, 512)
    vec = pl.BlockSpec((1, d), lambda i: (0, 0))
    lanes = pl.BlockSpec((tm, LANES), lambda i: (i, 0))
    return pl.pallas_call(
        functools.partial(_outproj_kernel, n_experts=n_experts),
        grid=(t // tm,),
        in_specs=[pl.BlockSpec((tm, wa), lambda i: (i, 0)), pl.BlockSpec((1, wa), lambda i: (0, 0)),
                  pl.BlockSpec((tm, wr), lambda i: (i, 0)),
                  pl.BlockSpec((wa, d), lambda i: (0, 0)), pl.BlockSpec((wr, d), lambda i: (0, 0)),
                  pl.BlockSpec((tm, d), lambda i: (i, 0)), vec, vec, vec, vec,
                  pl.BlockSpec((d, 2 * LANES), lambda i: (0, 0)), pl.BlockSpec((1, LANES), lambda i: (0, 0))],
        out_specs=[pl.BlockSpec((tm, d), lambda i: (i, 0)), pl.BlockSpec((tm, d), lambda i: (i, 0)),
                   lanes, lanes],
        out_shape=[jax.ShapeDtypeStruct((t, d), F32), jax.ShapeDtypeStruct((t, d), F32),
                   jax.ShapeDtypeStruct((t, LANES), F32), jax.ShapeDtypeStruct((t, LANES), F32)],
        compiler_params=_cparams(("arbitrary",)),
        name="outproj",
    )(attn, attn_g, lru_n, wa_bf16, wr_bf16, x, g1, n2, sc, sh, rw_split, rb_pad)


def _route_kernel(ti_ref, dest_ref, cnt_ref, carry, gstart):
    ph = pl.program_id(0)
    i = pl.program_id(1)
    last = pl.num_programs(1) - 1
    tm = ti_ref.shape[0]
    ti = ti_ref[...]
    lane = lax.broadcasted_iota(jnp.int32, (tm, LANES), 1).astype(F32)
    sel = [lane == ti[:, k:k + 1] for k in range(TOP_K)]
    hot = jnp.zeros((tm, LANES), F32)
    for k in range(TOP_K):
        hot = jnp.where(sel[k], 1.0, hot)

    @pl.when(jnp.logical_and(ph == 0, i == 0))
    def _():
        carry[...] = jnp.zeros(carry.shape, F32)

    @pl.when(ph == 1)
    def _():
        row = lax.broadcasted_iota(jnp.int32, (tm, tm), 0)
        col = lax.broadcasted_iota(jnp.int32, (tm, tm), 1)
        tri = jnp.where(col < row, 1.0, 0.0).astype(BF16)
        before = jnp.dot(tri, hot.astype(BF16), preferred_element_type=F32) + carry[0:1, :]
        base = before + gstart[0:1, :]
        dest = jnp.zeros((tm, LANES), F32)
        for k in range(TOP_K):
            dk = jnp.sum(jnp.where(sel[k], base, 0.0), axis=-1, keepdims=True)
            dest = jnp.where(lane == k, dk, dest)
        dest_ref[...] = dest.astype(jnp.int32)

    carry[...] = carry[...] + jnp.sum(hot, axis=0, keepdims=True)

    @pl.when(jnp.logical_and(ph == 0, i == last))
    def _():
        cnt = carry[...]
        cnt_ref[...] = cnt[0:1, :]
        padded = jnp.ceil(cnt / MOE_SUB) * MOE_SUB
        lane8 = lax.broadcasted_iota(jnp.int32, cnt.shape, 1)
        acc = padded
        for s in (1, 2, 4, 8, 16, 32, 64):
            acc = acc + jnp.where(lane8 >= s, pltpu.roll(acc, s, axis=1), 0.0)
        gstart[...] = acc - padded
        carry[...] = jnp.zeros(carry.shape, F32)


def _route(topi):
    t = topi.shape[0]
    tm = min(t, TOK_TILE)
    dest, cnt = pl.pallas_call(
        _route_kernel,
        grid=(2, t // tm),
        in_specs=[pl.BlockSpec((tm, LANES), lambda p, i: (i, 0))],
        out_specs=[pl.BlockSpec((tm, LANES), lambda p, i: (i * p, 0)),
                   pl.BlockSpec((1, LANES), lambda p, i: (0, 0))],
        out_shape=[jax.ShapeDtypeStruct((t, LANES), jnp.int32), jax.ShapeDtypeStruct((1, LANES), F32)],
        scratch_shapes=[pltpu.VMEM((SUBLANES, LANES), F32), pltpu.VMEM((SUBLANES, LANES), F32)],
        compiler_params=_cparams(("arbitrary", "arbitrary")),
        name="route",
    )(topi)
    return dest, cnt


def _row_copy(src, dst, sem):
    return pltpu.make_async_copy(src, dst, sem)


def _dispatch_kernel(dest_sm, h_ref, xs_ref, sem):
    i = pl.program_id(0)
    tm = h_ref.shape[0]

    def issue(r, _):
        for k in range(TOP_K):
            d = dest_sm[(i * tm + r) * TOP_K + k]
            _row_copy(h_ref.at[pl.ds(r, 1), :], xs_ref.at[pl.ds(d, 1), :], sem).start()
        return 0

    lax.fori_loop(0, tm, issue, 0)
    for _ in range(TOP_K):
        _row_copy(h_ref, xs_ref.at[pl.ds(0, tm), :], sem).wait()


def _dispatch(dest_flat, h, n_rows):
    t, d = h.shape
    tm = min(t, TOK_TILE)
    return pl.pallas_call(
        _dispatch_kernel,
        grid_spec=pltpu.PrefetchScalarGridSpec(
            num_scalar_prefetch=1,
            grid=(t // tm,),
            in_specs=[pl.BlockSpec((tm, d), lambda i, dd: (i, 0))],
            out_specs=pl.BlockSpec(memory_space=pl.ANY),
            scratch_shapes=[pltpu.SemaphoreType.DMA],
        ),
        out_shape=jax.ShapeDtypeStruct((n_rows, d), F32),
        compiler_params=_cparams(("arbitrary",)),
        name="dispatch",
    )(dest_flat, h)


TAIL_PIECES = tuple(MOE_SUB >> (i + 1) for i in range(MOE_SUB.bit_length() - 1))


def _moe_kernel(ue_sm, ur_sm, ub_sm, na_sm, xs_ref, w1_ref, b1_ref, w2_ref, b2_ref, sel_ref, ys_ref,
                xbuf, acc, xsem, ysem):
    del ue_sm
    u = pl.program_id(0)
    f = pl.program_id(1)
    last_f = pl.num_programs(1) - 1
    n_act = na_sm[0]
    d = xbuf.shape[1]
    tf2 = w1_ref.shape[1]

    @pl.when(jnp.logical_and(u == 0, f == 0))
    def _():
        xbuf[...] = jnp.zeros(xbuf.shape, F32)

    def x_copy(base, off, n, slot):
        return pltpu.make_async_copy(xs_ref.at[pl.ds(base + off, n), :], xbuf.at[pl.ds(off, n), :], xsem.at[slot])

    def y_copy(base, bi):
        r0 = pl.multiple_of(bi * MOE_SUB, MOE_SUB)
        return pltpu.make_async_copy(acc.at[pl.ds(r0, MOE_SUB), :], ys_ref.at[pl.ds(base + r0, MOE_SUB), :],
                                     ysem.at[bi])

    def tail_pieces(base, n_full, rem, fn):
        off = n_full * MOE_SUB
        for p in TAIL_PIECES:
            take = (rem & p) != 0

            @pl.when(take)
            def _(off=off, p=p):
                fn(x_copy(base, pl.multiple_of(off, p), p, n_full))

            off = off + jnp.where(take, p, 0)

    @pl.when(u < n_act)
    def _():
        rows = ur_sm[u]
        base = pl.multiple_of(ub_sm[u], MOE_SUB)
        n_full = rows // MOE_SUB
        rem = rows % MOE_SUB
        n_sub = (rows + MOE_SUB - 1) // MOE_SUB

        @pl.when(f == 0)
        def _():
            @pl.when(u > 0)
            def _():
                prev_base = pl.multiple_of(ub_sm[u - 1], MOE_SUB)
                prev_sub = (ur_sm[u - 1] + MOE_SUB - 1) // MOE_SUB

                def drain(bi, _):
                    y_copy(prev_base, bi).wait()
                    return 0

                lax.fori_loop(0, prev_sub, drain, 0)

            def fetch(bi, _):
                x_copy(base, pl.multiple_of(bi * MOE_SUB, MOE_SUB), MOE_SUB, bi).start()
                return 0

            lax.fori_loop(0, n_full, fetch, 0)
            tail_pieces(base, n_full, rem, lambda c: c.start())

        def sub_block(bi, _):
            r0 = pl.multiple_of(bi * MOE_SUB, MOE_SUB)

            @pl.when(f == 0)
            def _():
                @pl.when(bi < n_full)
                def _():
                    x_copy(base, r0, MOE_SUB, bi).wait()

                @pl.when(bi >= n_full)
                def _():
                    tail_pieces(base, n_full, rem, lambda c: c.wait())

                acc[pl.ds(r0, MOE_SUB), :] = jnp.broadcast_to(b2_ref[...], (MOE_SUB, d))

            uu = jnp.dot(xbuf[pl.ds(r0, MOE_SUB), :], w1_ref[...], preferred_element_type=F32) + b1_ref[...]
            glu = jnp.minimum(uu, SWIGLU_LIMIT)
            glu = glu * _sigmoid(SWIGLU_ALPHA * glu)
            lin = jnp.clip(uu, -SWIGLU_LIMIT, SWIGLU_LIMIT) + 1.0
            acts = []
            for cidx in range(tf2 // (2 * LANES)):
                cs = slice(cidx * 2 * LANES, (cidx + 1) * 2 * LANES)
                prod = glu[:, cs] * pltpu.roll(lin[:, cs], 2 * LANES - 1, axis=1)
                acts.append(jnp.dot(prod.astype(BF16), sel_ref[...], preferred_element_type=F32))
            act = jnp.concatenate(acts, axis=1)
            acc[pl.ds(r0, MOE_SUB), :] += jnp.dot(act, w2_ref[...], preferred_element_type=F32)

            @pl.when(f == last_f)
            def _():
                y_copy(base, bi).start()

            return 0

        lax.fori_loop(0, n_sub, sub_block, 0)

        @pl.when(jnp.logical_and(f == last_f, u == n_act - 1))
        def _():
            def drain(bi, _):
                y_copy(base, bi).wait()
                return 0

            lax.fori_loop(0, n_sub, drain, 0)


def _moe(xs, w1, b1, w2, b2, layer, unit_expert, unit_rows, unit_base, n_active):
    n_rows, d = xs.shape
    n_layers, n_exp, _, ff2 = w1.shape
    ff = ff2 // 2
    n_units = unit_expert.shape[0]
    nf = ff // MOE_TF
    n_slots = MOE_UNIT // MOE_SUB
    sel = (jnp.arange(2 * LANES)[:, None] == 2 * jnp.arange(LANES)[None, :]).astype(BF16)

    def fstep(u, f, na):
        return jnp.where(u < na[0], f, nf - 1)

    return pl.pallas_call(
        _moe_kernel,
        grid_spec=pltpu.PrefetchScalarGridSpec(
            num_scalar_prefetch=4,
            grid=(n_units, nf),
            in_specs=[
                pl.BlockSpec(memory_space=pl.ANY),
                pl.BlockSpec((None, None, d, 2 * MOE_TF),
                             lambda u, f, ue, ur, ub, na: (layer, ue[u], 0, fstep(u, f, na))),
                pl.BlockSpec((None, None, 1, 2 * MOE_TF),
                             lambda u, f, ue, ur, ub, na: (layer, ue[u], 0, fstep(u, f, na))),
                pl.BlockSpec((None, None, MOE_TF, d),
                             lambda u, f, ue, ur, ub, na: (layer, ue[u], fstep(u, f, na), 0)),
                pl.BlockSpec((None, None, 1, d), lambda u, f, ue, ur, ub, na: (layer, ue[u], 0, 0)),
                pl.BlockSpec((2 * LANES, LANES), lambda u, f, ue, ur, ub, na: (0, 0)),
            ],
            out_specs=pl.BlockSpec(memory_space=pl.ANY),
            scratch_shapes=[pltpu.VMEM((MOE_UNIT, d), F32), pltpu.VMEM((MOE_UNIT, d), F32),
                            pltpu.SemaphoreType.DMA((n_slots,)), pltpu.SemaphoreType.DMA((n_slots,))],
        ),
        out_shape=jax.ShapeDtypeStruct((n_rows, d), F32),
        compiler_params=_cparams(("arbitrary", "arbitrary")),
        name="moe_experts",
    )(unit_expert, unit_rows, unit_base, n_active, xs, w1, b1.reshape(n_layers, n_exp, 1, ff2), w2,
      b2.reshape(n_layers, n_exp, 1, d), sel)


def _combine_kernel(dest_sm, ys_ref, tg_ref, x1_ref, g2_ref, fg_ref, o_ref, buf, sem, *, final_norm):
    i = pl.program_id(0)
    tm = x1_ref.shape[0]

    def issue(r, _):
        for k in range(TOP_K):
            d = dest_sm[(i * tm + r) * TOP_K + k]
            _row_copy(ys_ref.at[pl.ds(d, 1), :], buf.at[k, pl.ds(r, 1), :], sem).start()
        return 0

    lax.fori_loop(0, tm, issue, 0)
    for k in range(TOP_K):
        _row_copy(ys_ref.at[pl.ds(0, tm), :], buf.at[k], sem).wait()

    rows = 32

    def body(si, _):
        r0 = pl.multiple_of(si * rows, rows)
        tg = tg_ref[pl.ds(r0, rows), :]
        y = tg[:, 0:1] * buf[0, pl.ds(r0, rows), :]
        for k in range(1, TOP_K):
            y = y + tg[:, k:k + 1] * buf[k, pl.ds(r0, rows), :]
        x2 = x1_ref[pl.ds(r0, rows), :] + g2_ref[...] * y
        if final_norm:
            ms = jnp.mean(x2 * x2, axis=-1, keepdims=True)
            x2 = x2 * lax.rsqrt(ms + NORM_EPS) * fg_ref[...]
        o_ref[pl.ds(r0, rows), :] = x2
        return 0

    lax.fori_loop(0, tm // rows, body, 0)


def _combine(dest_flat, ys, tgate, x1, g2, final_g, final_norm):
    t, d = x1.shape
    tm = min(t, TOK_TILE)
    vec = pl.BlockSpec((1, d), lambda i, dd: (0, 0))
    return pl.pallas_call(
        functools.partial(_combine_kernel, final_norm=final_norm),
        grid_spec=pltpu.PrefetchScalarGridSpec(
            num_scalar_prefetch=1,
            grid=(t // tm,),
            in_specs=[pl.BlockSpec(memory_space=pl.ANY),
                      pl.BlockSpec((tm, LANES), lambda i, dd: (i, 0)),
                      pl.BlockSpec((tm, d), lambda i, dd: (i, 0)), vec, vec],
            out_specs=pl.BlockSpec((tm, d), lambda i, dd: (i, 0)),
            scratch_shapes=[pltpu.VMEM((TOP_K, tm, d), F32), pltpu.SemaphoreType.DMA],
        ),
        out_shape=jax.ShapeDtypeStruct((t, d), F32),
        compiler_params=_cparams(("arbitrary",)),
        name="moe_combine",
    )(dest_flat, ys, tgate, x1, g2, final_g)


def _unit_tables(counts, n_units):
    n_exp = counts.shape[0]
    padded = (counts + MOE_SUB - 1) // MOE_SUB * MOE_SUB
    gstart = jnp.cumsum(padded) - padded
    units_per = (counts + MOE_UNIT - 1) // MOE_UNIT
    ends = jnp.cumsum(units_per)
    starts = ends - units_per
    n_active = ends[-1]
    u = jnp.arange(n_units, dtype=jnp.int32)
    uc = jnp.minimum(u, n_active - 1)
    ue = jnp.minimum(jnp.sum((ends[None, :] <= uc[:, None]).astype(jnp.int32), axis=1), n_exp - 1)
    part = uc - starts[ue]
    rows = jnp.where(u < n_active, jnp.clip(counts[ue] - part * MOE_UNIT, 0, MOE_UNIT), 0)
    base = gstart[ue] + part * MOE_UNIT
    return ue.astype(jnp.int32), rows.astype(jnp.int32), base.astype(jnp.int32), n_active.reshape(1).astype(jnp.int32)


def kernel(x, c, positions, ada_w, ada_b, norm1_g, norm2_g, w_in, conv_w, conv_b, lru_wa, lru_ba, lru_wx,
           lru_bx, lru_lambda, attn_out_g, lru_out_g, w_out, router_w, router_b, w1, b1, w2, b2, final_g):
    bsz, seq, d = x.shape
    assert bsz == 1
    t = seq
    n_layers = ada_w.shape[0]
    attn_width = attn_out_g.shape[1]
    lru_width = lru_out_g.shape[1]
    n_experts = router_w.shape[2]
    assert attn_width == lru_width and w_in.shape[2] == 3 * attn_width + 2 * lru_width
    n_units = (t * TOP_K) // MOE_UNIT + n_experts
    n_rows = t * TOP_K + n_experts * MOE_SUB

    xf = x.reshape(t, d)
    pos_col = positions.reshape(t, 1)
    mod = _adaln(c, ada_w, ada_b)
    cos, sin = _rope_tables(pos_col)
    fg = final_g.reshape(1, d)

    for l in range(n_layers):
        sh1, sc1, g1, sh2, sc2, g2 = [mod[l, :, i * d:(i + 1) * d] for i in range(6)]
        z = _inproj(xf, norm1_g[l].reshape(1, d), sc1, sh1, w_in[l].astype(BF16), cos, sin, attn_width)
        attn = _attention(z, attn_width)
        wgate = jnp.concatenate([lru_wa[l], lru_wx[l]], axis=-1).astype(BF16)
        lru_n = _lru(z, pos_col, conv_w[l], conv_b[l].reshape(1, -1), wgate, lru_ba[l].reshape(1, -1),
                     lru_bx[l].reshape(1, -1), lru_lambda[l].reshape(1, -1), lru_out_g[l].reshape(1, -1),
                     3 * attn_width // lru_width)
        wo = w_out[l].astype(BF16)
        rw_pad = jnp.pad(router_w[l], ((0, 0), (0, LANES - n_experts)))
        rw_hi = rw_pad.astype(BF16)
        rw_split = jnp.concatenate([rw_hi, (rw_pad - rw_hi.astype(F32)).astype(BF16)], axis=1)
        rb_pad = jnp.pad(router_b[l], (0, LANES - n_experts)).reshape(1, LANES)
        x1, h, topi, tgate = _outproj(attn, attn_out_g[l].reshape(1, attn_width), lru_n, wo[:attn_width],
                                       wo[attn_width:], xf, g1,
                                       norm2_g[l].reshape(1, d), sc2, sh2, rw_split, rb_pad, n_experts)
        dest, cnt = _route(topi)
        counts = cnt[0, :n_experts].astype(jnp.int32)
        unit_expert, unit_rows, unit_base, n_active = _unit_tables(counts, n_units)
        dest_flat = dest[:, :TOP_K].reshape(t * TOP_K)
        xs = _dispatch(dest_flat, h, n_rows)
        ys = _moe(xs, w1, b1, w2, b2, l, unit_expert, unit_rows, unit_base, n_active)
        xf = _combine(dest_flat, ys, tgate, x1, g2, fg, l == n_layers - 1)
    return xf.reshape(bsz, seq, d)
```

```python
import functools

import jax
import jax.numpy as jnp
from jax import lax
from jax.experimental import pallas as pl
from jax.experimental.pallas import tpu as pltpu

F32 = jnp.float32
BF16 = jnp.bfloat16

HEAD_DIM = 128
LRU_BLOCK_W = 128
CONV_WIDTH = 4
LRU_C = 8.0
ROPE_THETA = 10000.0
DILATED_BRANCHES = ((128, 1), (512, 4), (2048, 16))
Q_BLOCK = 128
NEG_INF = -1e30
NORM_EPS = 1e-6
TOP_K = 4
SWIGLU_LIMIT = 7.0
SWIGLU_ALPHA = 1.702

LANES = 128
SUBLANES = 8
VMEM_LIMIT = 56 * 1024 * 1024

MOE_SUB = 256
MOE_UNIT = 1536
MOE_TF = 512
TOK_TILE = 256


def _cparams(sem, vmem=VMEM_LIMIT):
    return pltpu.CompilerParams(dimension_semantics=sem, vmem_limit_bytes=vmem)


def _sigmoid(x):
    return 0.5 * (1.0 + jnp.tanh(0.5 * x))


def _adaln_kernel(c_ref, w_ref, b_ref, o_ref, cond):
    d, tn = w_ref.shape
    groups = 4
    rows = groups * SUBLANES
    tiles = tn // LANES

    @pl.when(jnp.logical_and(pl.program_id(0) == 0, pl.program_id(1) == 0))
    def _():
        cv = c_ref[...]
        cond[...] = jnp.broadcast_to(cv * _sigmoid(cv), cond.shape)

    def body(i, accs):
        r = pl.multiple_of(i * rows, rows)
        out = []
        for g in range(groups):
            rg = r + g * SUBLANES
            cv = cond[pl.ds(rg, SUBLANES), :]
            for ti in range(tiles):
                out.append(accs[g * tiles + ti] + w_ref[pl.ds(rg, SUBLANES), ti * LANES:(ti + 1) * LANES] * cv)
        return tuple(out)

    zero = jnp.zeros((SUBLANES, LANES), F32)
    accs = lax.fori_loop(0, d // rows, body, tuple(zero for _ in range(groups * tiles)))
    for ti in range(tiles):
        acc = (accs[ti] + accs[tiles + ti]) + (accs[2 * tiles + ti] + accs[3 * tiles + ti])
        cols = slice(ti * LANES, (ti + 1) * LANES)
        o_ref[:, cols] = jnp.sum(acc, axis=0, keepdims=True) + b_ref[:, cols]


def _adaln(c, ada_w, ada_b):
    n_layers, d, n = ada_w.shape
    tn = 1024
    return pl.pallas_call(
        _adaln_kernel,
        grid=(n_layers, n // tn),
        in_specs=[pl.BlockSpec((d, 1), lambda l, j: (0, 0)),
                  pl.BlockSpec((None, d, tn), lambda l, j: (l, 0, j)),
                  pl.BlockSpec((None, 1, tn), lambda l, j: (l, 0, j))],
        out_specs=pl.BlockSpec((None, 1, tn), lambda l, j: (l, 0, j)),
        out_shape=jax.ShapeDtypeStruct((n_layers, 1, n), F32),
        scratch_shapes=[pltpu.VMEM((d, LANES), F32)],
        compiler_params=_cparams(("arbitrary", "arbitrary")),
        name="adaln",
    )(c.reshape(d, 1), ada_w, ada_b.reshape(n_layers, 1, n))


def _rope_kernel(pos_ref, invf_ref, cos_ref, sin_ref):
    ang = pos_ref[...].astype(F32) * invf_ref[...]
    cos_ref[...] = jnp.cos(ang)
    s = jnp.sin(ang)
    lane = lax.broadcasted_iota(jnp.int32, s.shape, 1)
    sin_ref[...] = jnp.where(lane < HEAD_DIM // 2, -s, s)


def _rope_tables(pos_col):
    t = pos_col.shape[0]
    tm = min(t, 1024)
    inv = ROPE_THETA ** (-jnp.arange(0, HEAD_DIM, 2, dtype=F32) / HEAD_DIM)
    inv = jnp.concatenate([inv, inv]).reshape(1, HEAD_DIM)
    return pl.pallas_call(
        _rope_kernel,
        grid=(t // tm,),
        in_specs=[pl.BlockSpec((tm, 1), lambda i: (i, 0)),
                  pl.BlockSpec((1, HEAD_DIM), lambda i: (0, 0))],
        out_specs=[pl.BlockSpec((tm, HEAD_DIM), lambda i: (i, 0))] * 2,
        out_shape=[jax.ShapeDtypeStruct((t, HEAD_DIM), F32)] * 2,
        compiler_params=_cparams(("arbitrary",)),
        name="rope_tables",
    )(pos_col, inv)


def _inproj_kernel(x_ref, g_ref, sc_ref, sh_ref, w_ref, cos_ref, sin_ref, o_ref, h_scr,
                   *, q_tiles, rope_tiles):
    j = pl.program_id(1)
    tm, tn = o_ref.shape
    rows = 32

    @pl.when(j == 0)
    def _():
        a = g_ref[...] * (1.0 + sc_ref[...])
        b = sh_ref[...]

        def body(i, _):
            r = pl.multiple_of(i * rows, rows)
            xv = x_ref[pl.ds(r, rows), :]
            ms = jnp.mean(xv * xv, axis=-1, keepdims=True)
            h_scr[pl.ds(r, rows), :] = (xv * lax.rsqrt(ms + NORM_EPS) * a + b).astype(BF16)
            return 0

        lax.fori_loop(0, tm // rows, body, 0)

    @pl.when(j < rope_tiles)
    def _():
        acc = jnp.dot(h_scr[...], w_ref[...], preferred_element_type=F32)
        scale = jnp.where(j < q_tiles, HEAD_DIM ** -0.5, 1.0).astype(F32)
        cs = cos_ref[...] * scale
        sn = sin_ref[...] * scale
        for c in range(tn // HEAD_DIM):
            cols = slice(c * HEAD_DIM, (c + 1) * HEAD_DIM)
            t = acc[:, cols]
            o_ref[:, cols] = (t * cs + pltpu.roll(t, HEAD_DIM // 2, axis=1) * sn).astype(BF16)

    @pl.when(j >= rope_tiles)
    def _():
        o_ref[...] = jnp.dot(h_scr[...], w_ref[...], preferred_element_type=F32).astype(BF16)


def _inproj(x, g, sc, sh, w_bf16, cos, sin, attn_width):
    t, d = x.shape
    n = w_bf16.shape[1]
    tm, tn = min(t, 1024), 512
    kern = functools.partial(_inproj_kernel, q_tiles=attn_width // tn, rope_tiles=2 * attn_width // tn)
    vec = pl.BlockSpec((1, d), lambda i, j: (0, 0))
    return pl.pallas_call(
        kern,
        grid=(t // tm, n // tn),
        in_specs=[pl.BlockSpec((tm, d), lambda i, j: (i, 0)), vec, vec, vec,
                  pl.BlockSpec((d, tn), lambda i, j: (0, j)),
                  pl.BlockSpec((tm, HEAD_DIM), lambda i, j: (i, 0)),
                  pl.BlockSpec((tm, HEAD_DIM), lambda i, j: (i, 0))],
        out_specs=pl.BlockSpec((tm, tn), lambda i, j: (i, j)),
        out_shape=jax.ShapeDtypeStruct((t, n), BF16),
        scratch_shapes=[pltpu.VMEM((tm, d), BF16)],
        compiler_params=_cparams(("arbitrary", "arbitrary")),
        name="inproj",
    )(x, g, sc, sh, w_bf16, cos, sin)


ATT_TILE = 2048
ATT_HEADS = 2
ATT_UNROLL = 4


def _rows(start, n, stride):
    return pl.ds(start, n) if stride == 1 else pl.ds(start, n, stride=stride)


def _attn_kernel(q_ref, kc_ref, kp_ref, vc_ref, vp_ref, o_ref, qf, kf, vf, oacc, lacc, *, dilations):
    m = pl.program_id(1)
    tp, wcols = q_ref.shape
    nh = wcols // HEAD_DIM
    w = Q_BLOCK
    n_blk = tp // w

    chunk = 256

    def widen(i, _):
        r = pl.multiple_of(i * chunk, chunk)
        for h in range(nh):
            cols = slice(h * HEAD_DIM, (h + 1) * HEAD_DIM)
            qf[h, pl.ds(r, chunk), :] = q_ref[pl.ds(r, chunk), cols].astype(F32)
            kf[h, pl.ds(r, chunk), :] = kp_ref[pl.ds(r, chunk), cols].astype(F32)
            kf[h, pl.ds(tp + r, chunk), :] = kc_ref[pl.ds(r, chunk), cols].astype(F32)
            vf[h, pl.ds(r, chunk), :] = vp_ref[pl.ds(r, chunk), cols].astype(F32)
            vf[h, pl.ds(tp + r, chunk), :] = vc_ref[pl.ds(r, chunk), cols].astype(F32)
        return 0

    lax.fori_loop(0, tp // chunk, widen, 0)

    qi = lax.broadcasted_iota(jnp.int32, (w, 2 * w), 0)
    kk = lax.broadcasted_iota(jnp.int32, (w, 2 * w), 1)
    band = jnp.logical_and(kk >= qi, kk <= qi + w)
    behind = kk >= w

    def block(bi, d, start, padded):
        start_k = start + tp - w * d
        for h in range(nh):
            q = qf[h, _rows(start, w, d), :].astype(BF16)
            k = kf[h, _rows(start_k, 2 * w, d), :].astype(BF16)
            v = vf[h, _rows(start_k, 2 * w, d), :].astype(BF16)
            s = lax.dot_general(q, k, (((1,), (1,)), ((), ())), preferred_element_type=F32)
            valid = jnp.logical_and(band, jnp.logical_or(behind, jnp.logical_not(padded)))
            s = jnp.where(valid, s, NEG_INF)
            mx = jnp.max(s, axis=-1, keepdims=True)
            p = jnp.exp(s - mx)
            den = jnp.sum(p, axis=-1, keepdims=True)
            o = jnp.dot(p.astype(BF16), v, preferred_element_type=F32) / den
            oacc[bi * nh + h, _rows(start, w, d), :] = o
            lacc[bi * nh + h, _rows(start, w, d), :] = jnp.broadcast_to(mx + jnp.log(den), (w, LANES))

    for bi, d in enumerate(dilations):
        per_class = n_blk // d

        def group(it, _, bi=bi, d=d, per_class=per_class):
            for j in range(ATT_UNROLL):
                idx = it * ATT_UNROLL + j
                r = idx // per_class
                nq = idx % per_class
                start = nq * (d * w) + r
                if d == 1:
                    start = pl.multiple_of(start, w)
                block(bi, d, start, jnp.logical_and(m == 0, nq == 0))
            return 0

        lax.fori_loop(0, n_blk // ATT_UNROLL, group, 0)

    rows = 64

    def mix(i, _):
        r = pl.multiple_of(i * rows, rows)
        for h in range(nh):
            cols = slice(h * HEAD_DIM, (h + 1) * HEAD_DIM)
            ls = [lacc[bi * nh + h, pl.ds(r, rows), :] for bi in range(len(dilations))]
            mx = functools.reduce(jnp.maximum, ls)
            es = [jnp.exp(l - mx) for l in ls]
            inv = 1.0 / functools.reduce(jnp.add, es)
            out = es[0] * inv * oacc[h, pl.ds(r, rows), :]
            for bi in range(1, len(dilations)):
                out = out + es[bi] * inv * oacc[bi * nh + h, pl.ds(r, rows), :]
            o_ref[pl.ds(r, rows), cols] = out.astype(o_ref.dtype)
        return 0

    lax.fori_loop(0, tp // rows, mix, 0)


def _attention(z, attn_width):
    t = z.shape[0]
    tp = ATT_TILE
    dilations = tuple(d for _, d in DILATED_BRANCHES)
    assert all(win // d == Q_BLOCK and win <= tp for win, d in DILATED_BRANCHES) and t % tp == 0
    wcols = ATT_HEADS * HEAD_DIM
    ng = attn_width // wcols

    def cur(off):
        return pl.BlockSpec((tp, wcols), lambda g, m: (m, off * ng + g))

    def prev(off):
        return pl.BlockSpec((tp, wcols), lambda g, m: (jnp.maximum(m - 1, 0), off * ng + g))

    nb = len(dilations)
    return pl.pallas_call(
        functools.partial(_attn_kernel, dilations=dilations),
        grid=(ng, t // tp),
        in_specs=[cur(0), cur(1), prev(1), cur(2), prev(2)],
        out_specs=pl.BlockSpec((tp, wcols), lambda g, m: (m, g)),
        out_shape=jax.ShapeDtypeStruct((t, attn_width), BF16),
        scratch_shapes=[pltpu.VMEM((ATT_HEADS, tp, HEAD_DIM), F32), pltpu.VMEM((ATT_HEADS, 2 * tp, HEAD_DIM), F32),
                        pltpu.VMEM((ATT_HEADS, 2 * tp, HEAD_DIM), F32),
                        pltpu.VMEM((nb * ATT_HEADS, tp, HEAD_DIM), F32),
                        pltpu.VMEM((nb * ATT_HEADS, tp, LANES), F32)],
        compiler_params=_cparams(("arbitrary", "arbitrary")),
        name="attention",
    )(z, z, z, z, z)


def _softplus(x):
    return jnp.maximum(x, 0.0) + jnp.log1p(jnp.exp(-jnp.abs(x)))


def _gelu_tanh(x):
    return 0.5 * x * (1.0 + jnp.tanh(0.7978845608028654 * (x + 0.044715 * x * x * x)))


def _lru_kernel(xr_ref, gr_ref, pos_ref, cw_ref, cb_ref, wg_ref, ba_ref, bx_ref, lam_ref, g_ref,
                o_ref, xbuf, a_scr, b_scr, hcar, *, n_blocks):
    i = pl.program_id(0)
    tc, width = a_scr.shape
    pad = SUBLANES
    bw = LRU_BLOCK_W

    @pl.when(i == 0)
    def _():
        xbuf[0:pad, :] = jnp.zeros((pad, width), F32)
        hcar[...] = jnp.zeros(hcar.shape, F32)

    xbuf[pad:, :] = xr_ref[...].astype(F32)
    reset = pos_ref[...] == 0
    sub = lax.broadcasted_iota(jnp.int32, (tc, bw), 0) % SUBLANES

    for hb in range(n_blocks):
        cols = slice(hb * bw, (hb + 1) * bw)
        xc = cb_ref[:, cols] + cw_ref[0:1, cols] * xbuf[pad - 3:pad - 3 + tc, cols]
        for k in range(1, CONV_WIDTH):
            xc = xc + cw_ref[k:k + 1, cols] * xbuf[pad - 3 + k:pad - 3 + k + tc, cols]
        gates = jnp.dot(xc.astype(BF16), wg_ref[hb], preferred_element_type=F32)
        r = _sigmoid(gates[:, :bw] + ba_ref[:, cols])
        ig = _sigmoid(gates[:, bw:] + bx_ref[:, cols])
        log_a = -LRU_C * r * _softplus(-lam_ref[:, cols])
        ea = jnp.exp(log_a)
        a = jnp.where(reset, 0.0, ea)
        mult = jnp.where(reset, 1.0, jnp.sqrt(-jnp.tanh(log_a) * (ea * ea + 1.0)))
        b = xc * ig * mult
        for s in (1, 2, 4):
            a_s = pltpu.roll(a, s, axis=0)
            b_s = pltpu.roll(b, s, axis=0)
            keep = sub >= s
            b = jnp.where(keep, a * b_s + b, b)
            a = jnp.where(keep, a * a_s, a)
        a_scr[:, cols] = a
        b_scr[:, cols] = b

    def group(gi, h_in):
        r0 = pl.multiple_of(gi * SUBLANES, SUBLANES)
        hh = a_scr[pl.ds(r0, SUBLANES), :] * h_in + b_scr[pl.ds(r0, SUBLANES), :]
        a_scr[pl.ds(r0, SUBLANES), :] = hh
        return jnp.broadcast_to(hh[SUBLANES - 1:SUBLANES, :], hh.shape)

    hcar[...] = lax.fori_loop(0, tc // SUBLANES, group, hcar[...])
    xbuf[0:pad, :] = xbuf[tc:tc + pad, :]

    rows = 32

    def epilogue(si, _):
        r0 = pl.multiple_of(si * rows, rows)
        y = a_scr[pl.ds(r0, rows), :] * _gelu_tanh(gr_ref[pl.ds(r0, rows), :].astype(F32))
        ms = jnp.mean(y * y, axis=-1, keepdims=True)
        o_ref[pl.ds(r0, rows), :] = (y * lax.rsqrt(ms + NORM_EPS) * g_ref[...]).astype(BF16)
        return 0

    lax.fori_loop(0, tc // rows, epilogue, 0)


def _lru(z, pos_col, conv_w, conv_b, wgate_bf16, ba, bx, lam, g, col_block):
    t = z.shape[0]
    width = conv_w.shape[1]
    tc = min(t, 128)
    n_blocks = width // LRU_BLOCK_W
    vec = pl.BlockSpec((1, width), lambda i: (0, 0))
    return pl.pallas_call(
        functools.partial(_lru_kernel, n_blocks=n_blocks),
        grid=(t // tc,),
        in_specs=[pl.BlockSpec((tc, width), lambda i: (i, col_block)),
                  pl.BlockSpec((tc, width), lambda i: (i, col_block + 1)),
                  pl.BlockSpec((tc, 1), lambda i: (i, 0)),
                  pl.BlockSpec((CONV_WIDTH, width), lambda i: (0, 0)), vec,
                  pl.BlockSpec((n_blocks, LRU_BLOCK_W, 2 * LRU_BLOCK_W), lambda i: (0, 0, 0)),
                  vec, vec, vec, vec],
        out_specs=pl.BlockSpec((tc, width), lambda i: (i, 0)),
        out_shape=jax.ShapeDtypeStruct((t, width), BF16),
        scratch_shapes=[pltpu.VMEM((tc + SUBLANES, width), F32), pltpu.VMEM((tc, width), F32),
                        pltpu.VMEM((tc, width), F32), pltpu.VMEM((SUBLANES, width), F32)],
        compiler_params=_cparams(("arbitrary",)),
        name="rg_lru",
    )(z, z, pos_col, conv_w, conv_b, wgate_bf16, ba, bx, lam, g)


def _outproj_kernel(a_ref, ag_ref, r_ref, wa_ref, wr_ref, x_ref, g1_ref, n2_ref, sc_ref, sh_ref, rw_ref, rb_ref,
                    x1_ref, h_ref, ti_ref, tg_ref, *, n_experts):
    a = a_ref[...].astype(F32)
    a = a * lax.rsqrt(jnp.mean(a * a, axis=-1, keepdims=True) + NORM_EPS) * ag_ref[...]
    y = jnp.dot(a.astype(BF16), wa_ref[...], preferred_element_type=F32)
    y = y + jnp.dot(r_ref[...], wr_ref[...], preferred_element_type=F32)
    x1 = x_ref[...] + g1_ref[...] * y
    x1_ref[...] = x1
    ms = jnp.mean(x1 * x1, axis=-1, keepdims=True)
    h = x1 * lax.rsqrt(ms + NORM_EPS) * (n2_ref[...] * (1.0 + sc_ref[...])) + sh_ref[...]
    h_ref[...] = h
    h_hi = h.astype(BF16)
    h_lo = (h - h_hi.astype(F32)).astype(BF16)
    p1 = jnp.dot(h_hi, rw_ref[...], preferred_element_type=F32)
    p2 = jnp.dot(h_lo, rw_ref[:, 0:LANES], preferred_element_type=F32)
    logits = p1[:, 0:LANES] + p1[:, LANES:] + p2 + rb_ref[...]
    lane = lax.broadcasted_iota(jnp.int32, logits.shape, 1).astype(F32)
    cur = jnp.where(lane < n_experts, logits, -jnp.inf)
    vals, idxs = [], []
    for _ in range(TOP_K):
        m = jnp.max(cur, axis=-1, keepdims=True)
        idx = jnp.min(jnp.where(cur == m, lane, float(LANES)), axis=-1, keepdims=True)
        vals.append(m)
        idxs.append(idx)
        cur = jnp.where(lane == idx, -jnp.inf, cur)
    es = [jnp.exp(v - vals[0]) for v in vals]
    inv = 1.0 / (es[0] + es[1] + es[2] + es[3])
    ti = jnp.zeros(logits.shape, F32)
    tg = jnp.zeros(logits.shape, F32)
    for k in range(TOP_K):
        ti = jnp.where(lane == k, idxs[k], ti)
        tg = jnp.where(lane == k, es[k] * inv, tg)
    ti_ref[...] = ti
    tg_ref[...] = tg


def _outproj(attn, attn_g, lru_n, wa_bf16, wr_bf16, x, g1, n2, sc, sh, rw_split, rb_pad, n_experts):
    t, d = x.shape
    wa = attn.shape[1]
    wr = lru_n.shape[1]
    tm = min(t, 512)
    vec = pl.BlockSpec((1, d), lambda i: (0, 0))
    lanes = pl.BlockSpec((tm, LANES), lambda i: (i, 0))
    return pl.pallas_call(
        functools.partial(_outproj_kernel, n_experts=n_experts),
        grid=(t // tm,),
        in_specs=[pl.BlockSpec((tm, wa), lambda i: (i, 0)), pl.BlockSpec((1, wa), lambda i: (0, 0)),
                  pl.BlockSpec((tm, wr), lambda i: (i, 0)),
                  pl.BlockSpec((wa, d), lambda i: (0, 0)), pl.BlockSpec((wr, d), lambda i: (0, 0)),
                  pl.BlockSpec((tm, d), lambda i: (i, 0)), vec, vec, vec, vec,
                  pl.BlockSpec((d, 2 * LANES), lambda i: (0, 0)), pl.BlockSpec((1, LANES), lambda i: (0, 0))],
        out_specs=[pl.BlockSpec((tm, d), lambda i: (i, 0)), pl.BlockSpec((tm, d), lambda i: (i, 0)),
                   lanes, lanes],
        out_shape=[jax.ShapeDtypeStruct((t, d), F32), jax.ShapeDtypeStruct((t, d), F32),
                   jax.ShapeDtypeStruct((t, LANES), F32), jax.ShapeDtypeStruct((t, LANES), F32)],
        compiler_params=_cparams(("arbitrary",)),
        name="outproj",
    )(attn, attn_g, lru_n, wa_bf16, wr_bf16, x, g1, n2, sc, sh, rw_split, rb_pad)


def _route_kernel(ti_ref, dest_ref, cnt_ref, carry, gstart):
    ph = pl.program_id(0)
    i = pl.program_id(1)
    last = pl.num_programs(1) - 1
    tm = ti_ref.shape[0]
    ti = ti_ref[...]
    lane = lax.broadcasted_iota(jnp.int32, (tm, LANES), 1).astype(F32)
    sel = [lane == ti[:, k:k + 1] for k in range(TOP_K)]
    hot = jnp.zeros((tm, LANES), F32)
    for k in range(TOP_K):
        hot = jnp.where(sel[k], 1.0, hot)

    @pl.when(jnp.logical_and(ph == 0, i == 0))
    def _():
        carry[...] = jnp.zeros(carry.shape, F32)

    @pl.when(ph == 1)
    def _():
        row = lax.broadcasted_iota(jnp.int32, (tm, tm), 0)
        col = lax.broadcasted_iota(jnp.int32, (tm, tm), 1)
        tri = jnp.where(col < row, 1.0, 0.0).astype(BF16)
        before = jnp.dot(tri, hot.astype(BF16), preferred_element_type=F32) + carry[0:1, :]
        base = before + gstart[0:1, :]
        dest = jnp.zeros((tm, LANES), F32)
        for k in range(TOP_K):
            dk = jnp.sum(jnp.where(sel[k], base, 0.0), axis=-1, keepdims=True)
            dest = jnp.where(lane == k, dk, dest)
        dest_ref[...] = dest.astype(jnp.int32)

    carry[...] = carry[...] + jnp.sum(hot, axis=0, keepdims=True)

    @pl.when(jnp.logical_and(ph == 0, i == last))
    def _():
        cnt = carry[...]
        cnt_ref[...] = cnt[0:1, :]
        padded = jnp.ceil(cnt / MOE_SUB) * MOE_SUB
        lane8 = lax.broadcasted_iota(jnp.int32, cnt.shape, 1)
        acc = padded
        for s in (1, 2, 4, 8, 16, 32, 64):
            acc = acc + jnp.where(lane8 >= s, pltpu.roll(acc, s, axis=1), 0.0)
        gstart[...] = acc - padded
        carry[...] = jnp.zeros(carry.shape, F32)


def _route(topi):
    t = topi.shape[0]
    tm = min(t, TOK_TILE)
    dest, cnt = pl.pallas_call(
        _route_kernel,
        grid=(2, t // tm),
        in_specs=[pl.BlockSpec((tm, LANES), lambda p, i: (i, 0))],
        out_specs=[pl.BlockSpec((tm, LANES), lambda p, i: (i * p, 0)),
                   pl.BlockSpec((1, LANES), lambda p, i: (0, 0))],
        out_shape=[jax.ShapeDtypeStruct((t, LANES), jnp.int32), jax.ShapeDtypeStruct((1, LANES), F32)],
        scratch_shapes=[pltpu.VMEM((SUBLANES, LANES), F32), pltpu.VMEM((SUBLANES, LANES), F32)],
        compiler_params=_cparams(("arbitrary", "arbitrary")),
        name="route",
    )(topi)
    return dest, cnt


def _row_copy(src, dst, sem):
    return pltpu.make_async_copy(src, dst, sem)


def _dispatch_kernel(dest_sm, h_ref, xs_ref, sem):
    i = pl.program_id(0)
    tm = h_ref.shape[0]

    def issue(r, _):
        for k in range(TOP_K):
            d = dest_sm[(i * tm + r) * TOP_K + k]
            _row_copy(h_ref.at[pl.ds(r, 1), :], xs_ref.at[pl.ds(d, 1), :], sem).start()
        return 0

    lax.fori_loop(0, tm, issue, 0)
    for _ in range(TOP_K):
        _row_copy(h_ref, xs_ref.at[pl.ds(0, tm), :], sem).wait()


def _dispatch(dest_flat, h, n_rows):
    t, d = h.shape
    tm = min(t, TOK_TILE)
    return pl.pallas_call(
        _dispatch_kernel,
        grid_spec=pltpu.PrefetchScalarGridSpec(
            num_scalar_prefetch=1,
            grid=(t // tm,),
            in_specs=[pl.BlockSpec((tm, d), lambda i, dd: (i, 0))],
            out_specs=pl.BlockSpec(memory_space=pl.ANY),
            scratch_shapes=[pltpu.SemaphoreType.DMA],
        ),
        out_shape=jax.ShapeDtypeStruct((n_rows, d), F32),
        compiler_params=_cparams(("arbitrary",)),
        name="dispatch",
    )(dest_flat, h)


TAIL_PIECES = tuple(MOE_SUB >> (i + 1) for i in range(MOE_SUB.bit_length() - 1))


def _moe_kernel(ue_sm, ur_sm, ub_sm, na_sm, xs_ref, w1_ref, b1_ref, w2_ref, b2_ref, sel_ref, ys_ref,
                xbuf, acc, xsem, ysem):
    del ue_sm
    u = pl.program_id(0)
    f = pl.program_id(1)
    last_f = pl.num_programs(1) - 1
    n_act = na_sm[0]
    d = xbuf.shape[1]
    tf2 = w1_ref.shape[1]

    @pl.when(jnp.logical_and(u == 0, f == 0))
    def _():
        xbuf[...] = jnp.zeros(xbuf.shape, F32)

    def x_copy(base, off, n, slot):
        return pltpu.make_async_copy(xs_ref.at[pl.ds(base + off, n), :], xbuf.at[pl.ds(off, n), :], xsem.at[slot])

    def y_copy(base, bi):
        r0 = pl.multiple_of(bi * MOE_SUB, MOE_SUB)
        return pltpu.make_async_copy(acc.at[pl.ds(r0, MOE_SUB), :], ys_ref.at[pl.ds(base + r0, MOE_SUB), :],
                                     ysem.at[bi])

    def tail_pieces(base, n_full, rem, fn):
        off = n_full * MOE_SUB
        for p in TAIL_PIECES:
            take = (rem & p) != 0

            @pl.when(take)
            def _(off=off, p=p):
                fn(x_copy(base, pl.multiple_of(off, p), p, n_full))

            off = off + jnp.where(take, p, 0)

    @pl.when(u < n_act)
    def _():
        rows = ur_sm[u]
        base = pl.multiple_of(ub_sm[u], MOE_SUB)
        n_full = rows // MOE_SUB
        rem = rows % MOE_SUB
        n_sub = (rows + MOE_SUB - 1) // MOE_SUB

        @pl.when(f == 0)
        def _():
            @pl.when(u > 0)
            def _():
                prev_base = pl.multiple_of(ub_sm[u - 1], MOE_SUB)
                prev_sub = (ur_sm[u - 1] + MOE_SUB - 1) // MOE_SUB

                def drain(bi, _):
                    y_copy(prev_base, bi).wait()
                    return 0

                lax.fori_loop(0, prev_sub, drain, 0)

            def fetch(bi, _):
                x_copy(base, pl.multiple_of(bi * MOE_SUB, MOE_SUB), MOE_SUB, bi).start()
                return 0

            lax.fori_loop(0, n_full, fetch, 0)
            tail_pieces(base, n_full, rem, lambda c: c.start())

        def sub_block(bi, _):
            r0 = pl.multiple_of(bi * MOE_SUB, MOE_SUB)

            @pl.when(f == 0)
            def _():
                @pl.when(bi < n_full)
                def _():
                    x_copy(base, r0, MOE_SUB, bi).wait()

                @pl.when(bi >= n_full)
                def _():
                    tail_pieces(base, n_full, rem, lambda c: c.wait())

                acc[pl.ds(r0, MOE_SUB), :] = jnp.broadcast_to(b2_ref[...], (MOE_SUB, d))

            xb = xbuf[pl.ds(r0, MOE_SUB), :].astype(BF16)
            uu = jnp.dot(xb, w1_ref[...].astype(BF16), preferred_element_type=F32) + b1_ref[...]
            glu = jnp.minimum(uu, SWIGLU_LIMIT)
            glu = glu * _sigmoid(SWIGLU_ALPHA * glu)
            lin = jnp.clip(uu, -SWIGLU_LIMIT, SWIGLU_LIMIT) + 1.0
            acts = []
            for cidx in range(tf2 // (2 * LANES)):
                cs = slice(cidx * 2 * LANES, (cidx + 1) * 2 * LANES)
                prod = glu[:, cs] * pltpu.roll(lin[:, cs], 2 * LANES - 1, axis=1)
                acts.append(jnp.dot(prod.astype(BF16), sel_ref[...], preferred_element_type=F32))
            act = jnp.concatenate(acts, axis=1).astype(BF16)
            acc[pl.ds(r0, MOE_SUB), :] += jnp.dot(act, w2_ref[...].astype(BF16), preferred_element_type=F32)

            @pl.when(f == last_f)
            def _():
                y_copy(base, bi).start()

            return 0

        lax.fori_loop(0, n_sub, sub_block, 0)

        @pl.when(jnp.logical_and(f == last_f, u == n_act - 1))
        def _():
            def drain(bi, _):
                y_copy(base, bi).wait()
                return 0

            lax.fori_loop(0, n_sub, drain, 0)


def _moe(xs, w1, b1, w2, b2, layer, unit_expert, unit_rows, unit_base, n_active):
    n_rows, d = xs.shape
    n_layers, n_exp, _, ff2 = w1.shape
    ff = ff2 // 2
    n_units = unit_expert.shape[0]
    nf = ff // MOE_TF
    n_slots = MOE_UNIT // MOE_SUB
    sel = (jnp.arange(2 * LANES)[:, None] == 2 * jnp.arange(LANES)[None, :]).astype(BF16)

    def fstep(u, f, na):
        return jnp.where(u < na[0], f, nf - 1)

    return pl.pallas_call(
        _moe_kernel,
        grid_spec=pltpu.PrefetchScalarGridSpec(
            num_scalar_prefetch=4,
            grid=(n_units, nf),
            in_specs=[
                pl.BlockSpec(memory_space=pl.ANY),
                pl.BlockSpec((None, None, d, 2 * MOE_TF),
                             lambda u, f, ue, ur, ub, na: (layer, ue[u], 0, fstep(u, f, na))),
                pl.BlockSpec((None, None, 1, 2 * MOE_TF),
                             lambda u, f, ue, ur, ub, na: (layer, ue[u], 0, fstep(u, f, na))),
                pl.BlockSpec((None, None, MOE_TF, d),
                             lambda u, f, ue, ur, ub, na: (layer, ue[u], fstep(u, f, na), 0)),
                pl.BlockSpec((None, None, 1, d), lambda u, f, ue, ur, ub, na: (layer, ue[u], 0, 0)),
                pl.BlockSpec((2 * LANES, LANES), lambda u, f, ue, ur, ub, na: (0, 0)),
            ],
            out_specs=pl.BlockSpec(memory_space=pl.ANY),
            scratch_shapes=[pltpu.VMEM((MOE_UNIT, d), F32), pltpu.VMEM((MOE_UNIT, d), F32),
                            pltpu.SemaphoreType.DMA((n_slots,)), pltpu.SemaphoreType.DMA((n_slots,))],
        ),
        out_shape=jax.ShapeDtypeStruct((n_rows, d), F32),
        compiler_params=_cparams(("arbitrary", "arbitrary")),
        name="moe_experts",
    )(unit_expert, unit_rows, unit_base, n_active, xs, w1, b1.reshape(n_layers, n_exp, 1, ff2), w2,
      b2.reshape(n_layers, n_exp, 1, d), sel)


def _combine_kernel(dest_sm, ys_ref, tg_ref, x1_ref, g2_ref, fg_ref, o_ref, buf, sem, *, final_norm):
    i = pl.program_id(0)
    tm = x1_ref.shape[0]

    def issue(r, _):
        for k in range(TOP_K):
            d = dest_sm[(i * tm + r) * TOP_K + k]
            _row_copy(ys_ref.at[pl.ds(d, 1), :], buf.at[k, pl.ds(r, 1), :], sem).start()
        return 0

    lax.fori_loop(0, tm, issue, 0)
    for k in range(TOP_K):
        _row_copy(ys_ref.at[pl.ds(0, tm), :], buf.at[k], sem).wait()

    rows = 32

    def body(si, _):
        r0 = pl.multiple_of(si * rows, rows)
        tg = tg_ref[pl.ds(r0, rows), :]
        y = tg[:, 0:1] * buf[0, pl.ds(r0, rows), :]
        for k in range(1, TOP_K):
            y = y + tg[:, k:k + 1] * buf[k, pl.ds(r0, rows), :]
        x2 = x1_ref[pl.ds(r0, rows), :] + g2_ref[...] * y
        if final_norm:
            ms = jnp.mean(x2 * x2, axis=-1, keepdims=True)
            x2 = x2 * lax.rsqrt(ms + NORM_EPS) * fg_ref[...]
        o_ref[pl.ds(r0, rows), :] = x2
        return 0

    lax.fori_loop(0, tm // rows, body, 0)


def _combine(dest_flat, ys, tgate, x1, g2, final_g, final_norm):
    t, d = x1.shape
    tm = min(t, TOK_TILE)
    vec = pl.BlockSpec((1, d), lambda i, dd: (0, 0))
    return pl.pallas_call(
        functools.partial(_combine_kernel, final_norm=final_norm),
        grid_spec=pltpu.PrefetchScalarGridSpec(
            num_scalar_prefetch=1,
            grid=(t // tm,),
            in_specs=[pl.BlockSpec(memory_space=pl.ANY),
                      pl.BlockSpec((tm, LANES), lambda i, dd: (i, 0)),
                      pl.BlockSpec((tm, d), lambda i, dd: (i, 0)), vec, vec],
            out_specs=pl.BlockSpec((tm, d), lambda i, dd: (i, 0)),
            scratch_shapes=[pltpu.VMEM((TOP_K, tm, d), F32), pltpu.SemaphoreType.DMA],
        ),
        out_shape=jax.ShapeDtypeStruct((t, d), F32),
        compiler_params=_cparams(("arbitrary",)),
        name="moe_combine",
    )(dest_flat, ys, tgate, x1, g2, final_g)


def _unit_tables(counts, n_units):
    n_exp = counts.shape[0]
    padded = (counts + MOE_SUB - 1) // MOE_SUB * MOE_SUB
    gstart = jnp.cumsum(padded) - padded
    units_per = (counts + MOE_UNIT - 1) // MOE_UNIT
    ends = jnp.cumsum(units_per)
    starts = ends - units_per
    n_active = ends[-1]
    u = jnp.arange(n_units, dtype=jnp.int32)
    uc = jnp.minimum(u, n_active - 1)
    ue = jnp.minimum(jnp.sum((ends[None, :] <= uc[:, None]).astype(jnp.int32), axis=1), n_exp - 1)
    part = uc - starts[ue]
    rows = jnp.where(u < n_active, jnp.clip(counts[ue] - part * MOE_UNIT, 0, MOE_UNIT), 0)
    base = gstart[ue] + part * MOE_UNIT
    return ue.astype(jnp.int32), rows.astype(jnp.int32), base.astype(jnp.int32), n_active.reshape(1).astype(jnp.int32)


def kernel(x, c, positions, ada_w, ada_b, norm1_g, norm2_g, w_in, conv_w, conv_b, lru_wa, lru_ba, lru_wx,
           lru_bx, lru_lambda, attn_out_g, lru_out_g, w_out, router_w, router_b, w1, b1, w2, b2, final_g):
    bsz, seq, d = x.shape
    assert bsz == 1
    t = seq
    n_layers = ada_w.shape[0]
    attn_width = attn_out_g.shape[1]
    lru_width = lru_out_g.shape[1]
    n_experts = router_w.shape[2]
    assert attn_width == lru_width and w_in.shape[2] == 3 * attn_width + 2 * lru_width
    n_units = (t * TOP_K) // MOE_UNIT + n_experts
    n_rows = t * TOP_K + n_experts * MOE_SUB

    xf = x.reshape(t, d)
    pos_col = positions.reshape(t, 1)
    mod = _adaln(c, ada_w, ada_b)
    cos, sin = _rope_tables(pos_col)
    fg = final_g.reshape(1, d)

    for l in range(n_layers):
        sh1, sc1, g1, sh2, sc2, g2 = [mod[l, :, i * d:(i + 1) * d] for i in range(6)]
        z = _inproj(xf, norm1_g[l].reshape(1, d), sc1, sh1, w_in[l].astype(BF16), cos, sin, attn_width)
        attn = _attention(z, attn_width)
        wgate = jnp.concatenate([lru_wa[l], lru_wx[l]], axis=-1).astype(BF16)
        lru_n = _lru(z, pos_col, conv_w[l], conv_b[l].reshape(1, -1), wgate, lru_ba[l].reshape(1, -1),
                     lru_bx[l].reshape(1, -1), lru_lambda[l].reshape(1, -1), lru_out_g[l].reshape(1, -1),
                     3 * attn_width // lru_width)
        wo = w_out[l].astype(BF16)
        rw_pad = jnp.pad(router_w[l], ((0, 0), (0, LANES - n_experts)))
        rw_hi = rw_pad.astype(BF16)
        rw_split = jnp.concatenate([rw_hi, (rw_pad - rw_hi.astype(F32)).astype(BF16)], axis=1)
        rb_pad = jnp.pad(router_b[l], (0, LANES - n_experts)).reshape(1, LANES)
        x1, h, topi, tgate = _outproj(attn, attn_out_g[l].reshape(1, attn_width), lru_n, wo[:attn_width],
                                       wo[attn_width:], xf, g1,
                                       norm2_g[l].reshape(1, d), sc2, sh2, rw_split, rb_pad, n_experts)
        dest, cnt = _route(topi)
        counts = cnt[0, :n_experts].astype(jnp.int32)
        unit_expert, unit_rows, unit_base, n_active = _unit_tables(counts, n_units)
        dest_flat = dest[:, :TOP_K].reshape(t * TOP_K)
        xs = _dispatch(dest_flat, h, n_rows)
        ys = _moe(xs, w1, b1, w2, b2, l, unit_expert, unit_rows, unit_base, n_active)
        xf = _combine(dest_flat, ys, tgate, x1, g2, fg, l == n_layers - 1)
    return xf.reshape(bsz, seq, d)
```

```python
import functools

import jax
import jax.numpy as jnp
from jax import lax
from jax.experimental import pallas as pl
from jax.experimental.pallas import tpu as pltpu

F32 = jnp.float32
BF16 = jnp.bfloat16

HEAD_DIM = 128
LRU_BLOCK_W = 128
CONV_WIDTH = 4
LRU_C = 8.0
ROPE_THETA = 10000.0
DILATED_BRANCHES = ((128, 1), (512, 4), (2048, 16))
Q_BLOCK = 128
NEG_INF = -1e30
NORM_EPS = 1e-6
TOP_K = 4
SWIGLU_LIMIT = 7.0
SWIGLU_ALPHA = 1.702

LANES = 128
SUBLANES = 8
VMEM_LIMIT = 56 * 1024 * 1024

MOE_SUB = 256
MOE_UNIT = 1280
MOE_TF = 512
TOK_TILE = 256


def _cparams(sem, vmem=VMEM_LIMIT):
    return pltpu.CompilerParams(dimension_semantics=sem, vmem_limit_bytes=vmem)


def _sigmoid(x):
    return 0.5 * (1.0 + jnp.tanh(0.5 * x))


def _adaln_kernel(c_ref, w_ref, b_ref, o_ref, cond):
    d, tn = w_ref.shape
    groups = 4
    rows = groups * SUBLANES
    tiles = tn // LANES

    @pl.when(jnp.logical_and(pl.program_id(0) == 0, pl.program_id(1) == 0))
    def _():
        cv = c_ref[...]
        cond[...] = jnp.broadcast_to(cv * _sigmoid(cv), cond.shape)

    def body(i, accs):
        r = pl.multiple_of(i * rows, rows)
        out = []
        for g in range(groups):
            rg = r + g * SUBLANES
            cv = cond[pl.ds(rg, SUBLANES), :]
            for ti in range(tiles):
                out.append(accs[g * tiles + ti] + w_ref[pl.ds(rg, SUBLANES), ti * LANES:(ti + 1) * LANES] * cv)
        return tuple(out)

    zero = jnp.zeros((SUBLANES, LANES), F32)
    accs = lax.fori_loop(0, d // rows, body, tuple(zero for _ in range(groups * tiles)))
    for ti in range(tiles):
        acc = (accs[ti] + accs[tiles + ti]) + (accs[2 * tiles + ti] + accs[3 * tiles + ti])
        cols = slice(ti * LANES, (ti + 1) * LANES)
        o_ref[:, cols] = jnp.sum(acc, axis=0, keepdims=True) + b_ref[:, cols]


def _adaln(c, ada_w, ada_b):
    n_layers, d, n = ada_w.shape
    tn = 1024
    return pl.pallas_call(
        _adaln_kernel,
        grid=(n_layers, n // tn),
        in_specs=[pl.BlockSpec((d, 1), lambda l, j: (0, 0)),
                  pl.BlockSpec((None, d, tn), lambda l, j: (l, 0, j)),
                  pl.BlockSpec((None, 1, tn), lambda l, j: (l, 0, j))],
        out_specs=pl.BlockSpec((None, 1, tn), lambda l, j: (l, 0, j)),
        out_shape=jax.ShapeDtypeStruct((n_layers, 1, n), F32),
        scratch_shapes=[pltpu.VMEM((d, LANES), F32)],
        compiler_params=_cparams(("arbitrary", "arbitrary")),
        name="adaln",
    )(c.reshape(d, 1), ada_w, ada_b.reshape(n_layers, 1, n))


def _rope_kernel(pos_ref, invf_ref, cos_ref, sin_ref):
    ang = pos_ref[...].astype(F32) * invf_ref[...]
    cos_ref[...] = jnp.cos(ang)
    s = jnp.sin(ang)
    lane = lax.broadcasted_iota(jnp.int32, s.shape, 1)
    sin_ref[...] = jnp.where(lane < HEAD_DIM // 2, -s, s)


def _rope_tables(pos_col):
    t = pos_col.shape[0]
    tm = min(t, 1024)
    inv = ROPE_THETA ** (-jnp.arange(0, HEAD_DIM, 2, dtype=F32) / HEAD_DIM)
    inv = jnp.concatenate([inv, inv]).reshape(1, HEAD_DIM)
    return pl.pallas_call(
        _rope_kernel,
        grid=(t // tm,),
        in_specs=[pl.BlockSpec((tm, 1), lambda i: (i, 0)),
                  pl.BlockSpec((1, HEAD_DIM), lambda i: (0, 0))],
        out_specs=[pl.BlockSpec((tm, HEAD_DIM), lambda i: (i, 0))] * 2,
        out_shape=[jax.ShapeDtypeStruct((t, HEAD_DIM), F32)] * 2,
        compiler_params=_cparams(("arbitrary",)),
        name="rope_tables",
    )(pos_col, inv)


def _inproj_kernel(x_ref, g_ref, sc_ref, sh_ref, w_ref, cos_ref, sin_ref, o_ref, h_scr,
                   *, q_tiles, rope_tiles):
    j = pl.program_id(1)
    tm, tn = o_ref.shape
    rows = 32

    @pl.when(j == 0)
    def _():
        a = g_ref[...] * (1.0 + sc_ref[...])
        b = sh_ref[...]

        def body(i, _):
            r = pl.multiple_of(i * rows, rows)
            xv = x_ref[pl.ds(r, rows), :]
            ms = jnp.mean(xv * xv, axis=-1, keepdims=True)
            h_scr[pl.ds(r, rows), :] = (xv * lax.rsqrt(ms + NORM_EPS) * a + b).astype(BF16)
            return 0

        lax.fori_loop(0, tm // rows, body, 0)

    @pl.when(j < rope_tiles)
    def _():
        acc = jnp.dot(h_scr[...], w_ref[...], preferred_element_type=F32)
        scale = jnp.where(j < q_tiles, HEAD_DIM ** -0.5, 1.0).astype(F32)
        cs = cos_ref[...] * scale
        sn = sin_ref[...] * scale
        for c in range(tn // HEAD_DIM):
            cols = slice(c * HEAD_DIM, (c + 1) * HEAD_DIM)
            t = acc[:, cols]
            o_ref[:, cols] = (t * cs + pltpu.roll(t, HEAD_DIM // 2, axis=1) * sn).astype(BF16)

    @pl.when(j >= rope_tiles)
    def _():
        o_ref[...] = jnp.dot(h_scr[...], w_ref[...], preferred_element_type=F32).astype(BF16)


def _inproj(x, g, sc, sh, w_bf16, cos, sin, attn_width):
    t, d = x.shape
    n = w_bf16.shape[1]
    tm, tn = min(t, 1024), 512
    kern = functools.partial(_inproj_kernel, q_tiles=attn_width // tn, rope_tiles=2 * attn_width // tn)
    vec = pl.BlockSpec((1, d), lambda i, j: (0, 0))
    return pl.pallas_call(
        kern,
        grid=(t // tm, n // tn),
        in_specs=[pl.BlockSpec((tm, d), lambda i, j: (i, 0)), vec, vec, vec,
                  pl.BlockSpec((d, tn), lambda i, j: (0, j)),
                  pl.BlockSpec((tm, HEAD_DIM), lambda i, j: (i, 0)),
                  pl.BlockSpec((tm, HEAD_DIM), lambda i, j: (i, 0))],
        out_specs=pl.BlockSpec((tm, tn), lambda i, j: (i, j)),
        out_shape=jax.ShapeDtypeStruct((t, n), BF16),
        scratch_shapes=[pltpu.VMEM((tm, d), BF16)],
        compiler_params=_cparams(("arbitrary", "arbitrary")),
        name="inproj",
    )(x, g, sc, sh, w_bf16, cos, sin)


ATT_TILE = 2048
ATT_HEADS = 2
ATT_UNROLL = 4


def _rows(start, n, stride):
    return pl.ds(start, n) if stride == 1 else pl.ds(start, n, stride=stride)


def _attn_kernel(q_ref, kc_ref, kp_ref, vc_ref, vp_ref, o_ref, qf, kf, vf, oacc, lacc, *, dilations):
    m = pl.program_id(1)
    tp, wcols = q_ref.shape
    nh = wcols // HEAD_DIM
    w = Q_BLOCK
    n_blk = tp // w

    chunk = 256

    def widen(i, _):
        r = pl.multiple_of(i * chunk, chunk)
        for h in range(nh):
            cols = slice(h * HEAD_DIM, (h + 1) * HEAD_DIM)
            qf[h, pl.ds(r, chunk), :] = q_ref[pl.ds(r, chunk), cols].astype(F32)
            kf[h, pl.ds(r, chunk), :] = kp_ref[pl.ds(r, chunk), cols].astype(F32)
            kf[h, pl.ds(tp + r, chunk), :] = kc_ref[pl.ds(r, chunk), cols].astype(F32)
            vf[h, pl.ds(r, chunk), :] = vp_ref[pl.ds(r, chunk), cols].astype(F32)
            vf[h, pl.ds(tp + r, chunk), :] = vc_ref[pl.ds(r, chunk), cols].astype(F32)
        return 0

    lax.fori_loop(0, tp // chunk, widen, 0)

    qi = lax.broadcasted_iota(jnp.int32, (w, 2 * w), 0)
    kk = lax.broadcasted_iota(jnp.int32, (w, 2 * w), 1)
    band = jnp.logical_and(kk >= qi, kk <= qi + w)
    behind = kk >= w

    def block(bi, d, start, padded):
        start_k = start + tp - w * d
        for h in range(nh):
            q = qf[h, _rows(start, w, d), :].astype(BF16)
            k = kf[h, _rows(start_k, 2 * w, d), :].astype(BF16)
            v = vf[h, _rows(start_k, 2 * w, d), :].astype(BF16)
            s = lax.dot_general(q, k, (((1,), (1,)), ((), ())), preferred_element_type=F32)
            valid = jnp.logical_and(band, jnp.logical_or(behind, jnp.logical_not(padded)))
            s = jnp.where(valid, s, NEG_INF)
            mx = jnp.max(s, axis=-1, keepdims=True)
            p = jnp.exp(s - mx)
            den = jnp.sum(p, axis=-1, keepdims=True)
            o = jnp.dot(p.astype(BF16), v, preferred_element_type=F32) / den
            oacc[bi * nh + h, _rows(start, w, d), :] = o
            lacc[bi * nh + h, _rows(start, w, d), :] = jnp.broadcast_to(mx + jnp.log(den), (w, LANES))

    for bi, d in enumerate(dilations):
        per_class = n_blk // d

        def group(it, _, bi=bi, d=d, per_class=per_class):
            for j in range(ATT_UNROLL):
                idx = it * ATT_UNROLL + j
                r = idx // per_class
                nq = idx % per_class
                start = nq * (d * w) + r
                if d == 1:
                    start = pl.multiple_of(start, w)
                block(bi, d, start, jnp.logical_and(m == 0, nq == 0))
            return 0

        lax.fori_loop(0, n_blk // ATT_UNROLL, group, 0)

    rows = 64

    def mix(i, _):
        r = pl.multiple_of(i * rows, rows)
        for h in range(nh):
            cols = slice(h * HEAD_DIM, (h + 1) * HEAD_DIM)
            ls = [lacc[bi * nh + h, pl.ds(r, rows), :] for bi in range(len(dilations))]
            mx = functools.reduce(jnp.maximum, ls)
            es = [jnp.exp(l - mx) for l in ls]
            inv = 1.0 / functools.reduce(jnp.add, es)
            out = es[0] * inv * oacc[h, pl.ds(r, rows), :]
            for bi in range(1, len(dilations)):
                out = out + es[bi] * inv * oacc[bi * nh + h, pl.ds(r, rows), :]
            o_ref[pl.ds(r, rows), cols] = out.astype(o_ref.dtype)
        return 0

    lax.fori_loop(0, tp // rows, mix, 0)


def _attention(z, attn_width):
    t = z.shape[0]
    tp = ATT_TILE
    dilations = tuple(d for _, d in DILATED_BRANCHES)
    assert all(win // d == Q_BLOCK and win <= tp for win, d in DILATED_BRANCHES) and t % tp == 0
    wcols = ATT_HEADS * HEAD_DIM
    ng = attn_width // wcols

    def cur(off):
        return pl.BlockSpec((tp, wcols), lambda g, m: (m, off * ng + g))

    def prev(off):
        return pl.BlockSpec((tp, wcols), lambda g, m: (jnp.maximum(m - 1, 0), off * ng + g))

    nb = len(dilations)
    return pl.pallas_call(
        functools.partial(_attn_kernel, dilations=dilations),
        grid=(ng, t // tp),
        in_specs=[cur(0), cur(1), prev(1), cur(2), prev(2)],
        out_specs=pl.BlockSpec((tp, wcols), lambda g, m: (m, g)),
        out_shape=jax.ShapeDtypeStruct((t, attn_width), BF16),
        scratch_shapes=[pltpu.VMEM((ATT_HEADS, tp, HEAD_DIM), F32), pltpu.VMEM((ATT_HEADS, 2 * tp, HEAD_DIM), F32),
                        pltpu.VMEM((ATT_HEADS, 2 * tp, HEAD_DIM), F32),
                        pltpu.VMEM((nb * ATT_HEADS, tp, HEAD_DIM), F32),
                        pltpu.VMEM((nb * ATT_HEADS, tp, LANES), F32)],
        compiler_params=_cparams(("arbitrary", "arbitrary")),
        name="attention",
    )(z, z, z, z, z)


def _softplus(x):
    return jnp.maximum(x, 0.0) + jnp.log1p(jnp.exp(-jnp.abs(x)))


def _gelu_tanh(x):
    return 0.5 * x * (1.0 + jnp.tanh(0.7978845608028654 * (x + 0.044715 * x * x * x)))


def _lru_kernel(xr_ref, gr_ref, pos_ref, cw_ref, cb_ref, wg_ref, ba_ref, bx_ref, lam_ref, g_ref,
                o_ref, xbuf, a_scr, b_scr, hcar, *, n_blocks):
    i = pl.program_id(0)
    tc, width = a_scr.shape
    pad = SUBLANES
    bw = LRU_BLOCK_W

    @pl.when(i == 0)
    def _():
        xbuf[0:pad, :] = jnp.zeros((pad, width), F32)
        hcar[...] = jnp.zeros(hcar.shape, F32)

    xbuf[pad:, :] = xr_ref[...].astype(F32)
    reset = pos_ref[...] == 0
    sub = lax.broadcasted_iota(jnp.int32, (tc, bw), 0) % SUBLANES

    for hb in range(n_blocks):
        cols = slice(hb * bw, (hb + 1) * bw)
        xc = cb_ref[:, cols] + cw_ref[0:1, cols] * xbuf[pad - 3:pad - 3 + tc, cols]
        for k in range(1, CONV_WIDTH):
            xc = xc + cw_ref[k:k + 1, cols] * xbuf[pad - 3 + k:pad - 3 + k + tc, cols]
        gates = jnp.dot(xc.astype(BF16), wg_ref[hb], preferred_element_type=F32)
        r = _sigmoid(gates[:, :bw] + ba_ref[:, cols])
        ig = _sigmoid(gates[:, bw:] + bx_ref[:, cols])
        log_a = -LRU_C * r * _softplus(-lam_ref[:, cols])
        ea = jnp.exp(log_a)
        a = jnp.where(reset, 0.0, ea)
        mult = jnp.where(reset, 1.0, jnp.sqrt(-jnp.tanh(log_a) * (ea * ea + 1.0)))
        b = xc * ig * mult
        for s in (1, 2, 4):
            a_s = pltpu.roll(a, s, axis=0)
            b_s = pltpu.roll(b, s, axis=0)
            keep = sub >= s
            b = jnp.where(keep, a * b_s + b, b)
            a = jnp.where(keep, a * a_s, a)
        a_scr[:, cols] = a
        b_scr[:, cols] = b

    def group(gi, h_in):
        r0 = pl.multiple_of(gi * SUBLANES, SUBLANES)
        hh = a_scr[pl.ds(r0, SUBLANES), :] * h_in + b_scr[pl.ds(r0, SUBLANES), :]
        a_scr[pl.ds(r0, SUBLANES), :] = hh
        return jnp.broadcast_to(hh[SUBLANES - 1:SUBLANES, :], hh.shape)

    hcar[...] = lax.fori_loop(0, tc // SUBLANES, group, hcar[...])
    xbuf[0:pad, :] = xbuf[tc:tc + pad, :]

    rows = 32

    def epilogue(si, _):
        r0 = pl.multiple_of(si * rows, rows)
        y = a_scr[pl.ds(r0, rows), :] * _gelu_tanh(gr_ref[pl.ds(r0, rows), :].astype(F32))
        ms = jnp.mean(y * y, axis=-1, keepdims=True)
        o_ref[pl.ds(r0, rows), :] = (y * lax.rsqrt(ms + NORM_EPS) * g_ref[...]).astype(BF16)
        return 0

    lax.fori_loop(0, tc // rows, epilogue, 0)


def _lru(z, pos_col, conv_w, conv_b, wgate_bf16, ba, bx, lam, g, col_block):
    t = z.shape[0]
    width = conv_w.shape[1]
    tc = min(t, 128)
    n_blocks = width // LRU_BLOCK_W
    vec = pl.BlockSpec((1, width), lambda i: (0, 0))
    return pl.pallas_call(
        functools.partial(_lru_kernel, n_blocks=n_blocks),
        grid=(t // tc,),
        in_specs=[pl.BlockSpec((tc, width), lambda i: (i, col_block)),
                  pl.BlockSpec((tc, width), lambda i: (i, col_block + 1)),
                  pl.BlockSpec((tc, 1), lambda i: (i, 0)),
                  pl.BlockSpec((CONV_WIDTH, width), lambda i: (0, 0)), vec,
                  pl.BlockSpec((n_blocks, LRU_BLOCK_W, 2 * LRU_BLOCK_W), lambda i: (0, 0, 0)),
                  vec, vec, vec, vec],
        out_specs=pl.BlockSpec((tc, width), lambda i: (i, 0)),
        out_shape=jax.ShapeDtypeStruct((t, width), BF16),
        scratch_shapes=[pltpu.VMEM((tc + SUBLANES, width), F32), pltpu.VMEM((tc, width), F32),
                        pltpu.VMEM((tc, width), F32), pltpu.VMEM((SUBLANES, width), F32)],
        compiler_params=_cparams(("arbitrary",)),
        name="rg_lru",
    )(z, z, pos_col, conv_w, conv_b, wgate_bf16, ba, bx, lam, g)


def _outproj_kernel(a_ref, ag_ref, r_ref, wa_ref, wr_ref, x_ref, g1_ref, n2_ref, sc_ref, sh_ref, rw_ref, rb_ref,
                    x1_ref, h_ref, ti_ref, tg_ref, *, n_experts):
    a = a_ref[...].astype(F32)
    a = a * lax.rsqrt(jnp.mean(a * a, axis=-1, keepdims=True) + NORM_EPS) * ag_ref[...]
    y = jnp.dot(a.astype(BF16), wa_ref[...], preferred_element_type=F32)
    y = y + jnp.dot(r_ref[...], wr_ref[...], preferred_element_type=F32)
    x1 = x_ref[...] + g1_ref[...] * y
    x1_ref[...] = x1
    ms = jnp.mean(x1 * x1, axis=-1, keepdims=True)
    h = x1 * lax.rsqrt(ms + NORM_EPS) * (n2_ref[...] * (1.0 + sc_ref[...])) + sh_ref[...]
    h_ref[...] = h
    h_hi = h.astype(BF16)
    h_lo = (h - h_hi.astype(F32)).astype(BF16)
    p1 = jnp.dot(h_hi, rw_ref[...], preferred_element_type=F32)
    p2 = jnp.dot(h_lo, rw_ref[:, 0:LANES], preferred_element_type=F32)
    logits = p1[:, 0:LANES] + p1[:, LANES:] + p2 + rb_ref[...]
    lane = lax.broadcasted_iota(jnp.int32, logits.shape, 1).astype(F32)
    cur = jnp.where(lane < n_experts, logits, -jnp.inf)
    vals, idxs = [], []
    for _ in range(TOP_K):
        m = jnp.max(cur, axis=-1, keepdims=True)
        idx = jnp.min(jnp.where(cur == m, lane, float(LANES)), axis=-1, keepdims=True)
        vals.append(m)
        idxs.append(idx)
        cur = jnp.where(lane == idx, -jnp.inf, cur)
    es = [jnp.exp(v - vals[0]) for v in vals]
    inv = 1.0 / (es[0] + es[1] + es[2] + es[3])
    ti = jnp.zeros(logits.shape, F32)
    tg = jnp.zeros(logits.shape, F32)
    for k in range(TOP_K):
        ti = jnp.where(lane == k, idxs[k], ti)
        tg = jnp.where(lane == k, es[k] * inv, tg)
    ti_ref[...] = ti
    tg_ref[...] = tg


def _outproj(attn, attn_g, lru_n, wa_bf16, wr_bf16, x, g1, n2, sc, sh, rw_split, rb_pad, n_experts):
    t, d = x.shape
    wa = attn.shape[1]
    wr = lru_n.shape[1]
    tm = min(t, 512)
    vec = pl.BlockSpec((1, d), lambda i: (0, 0))
    lanes = pl.BlockSpec((tm, LANES), lambda i: (i, 0))
    return pl.pallas_call(
        functools.partial(_outproj_kernel, n_experts=n_experts),
        grid=(t // tm,),
        in_specs=[pl.BlockSpec((tm, wa), lambda i: (i, 0)), pl.BlockSpec((1, wa), lambda i: (0, 0)),
                  pl.BlockSpec((tm, wr), lambda i: (i, 0)),
                  pl.BlockSpec((wa, d), lambda i: (0, 0)), pl.BlockSpec((wr, d), lambda i: (0, 0)),
                  pl.BlockSpec((tm, d), lambda i: (i, 0)), vec, vec, vec, vec,
                  pl.BlockSpec((d, 2 * LANES), lambda i: (0, 0)), pl.BlockSpec((1, LANES), lambda i: (0, 0))],
        out_specs=[pl.BlockSpec((tm, d), lambda i: (i, 0)), pl.BlockSpec((tm, d), lambda i: (i, 0)),
                   lanes, lanes],
        out_shape=[jax.ShapeDtypeStruct((t, d), F32), jax.ShapeDtypeStruct((t, d), F32),
                   jax.ShapeDtypeStruct((t, LANES), F32), jax.ShapeDtypeStruct((t, LANES), F32)],
        compiler_params=_cparams(("arbitrary",)),
        name="outproj",
    )(attn, attn_g, lru_n, wa_bf16, wr_bf16, x, g1, n2, sc, sh, rw_split, rb_pad)


def _route_kernel(ti_ref, dest_ref, cnt_ref, carry, gstart):
    ph = pl.program_id(0)
    i = pl.program_id(1)
    last = pl.num_programs(1) - 1
    tm = ti_ref.shape[0]
    ti = ti_ref[...]
    lane = lax.broadcasted_iota(jnp.int32, (tm, LANES), 1).astype(F32)
    sel = [lane == ti[:, k:k + 1] for k in range(TOP_K)]
    hot = jnp.zeros((tm, LANES), F32)
    for k in range(TOP_K):
        hot = jnp.where(sel[k], 1.0, hot)

    @pl.when(jnp.logical_and(ph == 0, i == 0))
    def _():
        carry[...] = jnp.zeros(carry.shape, F32)

    @pl.when(ph == 1)
    def _():
        row = lax.broadcasted_iota(jnp.int32, (tm, tm), 0)
        col = lax.broadcasted_iota(jnp.int32, (tm, tm), 1)
        tri = jnp.where(col < row, 1.0, 0.0).astype(BF16)
        before = jnp.dot(tri, hot.astype(BF16), preferred_element_type=F32) + carry[0:1, :]
        base = before + gstart[0:1, :]
        dest = jnp.zeros((tm, LANES), F32)
        for k in range(TOP_K):
            dk = jnp.sum(jnp.where(sel[k], base, 0.0), axis=-1, keepdims=True)
            dest = jnp.where(lane == k, dk, dest)
        dest_ref[...] = dest.astype(jnp.int32)

    carry[...] = carry[...] + jnp.sum(hot, axis=0, keepdims=True)

    @pl.when(jnp.logical_and(ph == 0, i == last))
    def _():
        cnt = carry[...]
        cnt_ref[...] = cnt[0:1, :]
        padded = jnp.ceil(cnt / MOE_SUB) * MOE_SUB
        lane8 = lax.broadcasted_iota(jnp.int32, cnt.shape, 1)
        acc = padded
        for s in (1, 2, 4, 8, 16, 32, 64):
            acc = acc + jnp.where(lane8 >= s, pltpu.roll(acc, s, axis=1), 0.0)
        gstart[...] = acc - padded
        carry[...] = jnp.zeros(carry.shape, F32)


def _route(topi):
    t = topi.shape[0]
    tm = min(t, TOK_TILE)
    dest, cnt = pl.pallas_call(
        _route_kernel,
        grid=(2, t // tm),
        in_specs=[pl.BlockSpec((tm, LANES), lambda p, i: (i, 0))],
        out_specs=[pl.BlockSpec((tm, LANES), lambda p, i: (i * p, 0)),
                   pl.BlockSpec((1, LANES), lambda p, i: (0, 0))],
        out_shape=[jax.ShapeDtypeStruct((t, LANES), jnp.int32), jax.ShapeDtypeStruct((1, LANES), F32)],
        scratch_shapes=[pltpu.VMEM((SUBLANES, LANES), F32), pltpu.VMEM((SUBLANES, LANES), F32)],
        compiler_params=_cparams(("arbitrary", "arbitrary")),
        name="route",
    )(topi)
    return dest, cnt


def _row_copy(src, dst, sem):
    return pltpu.make_async_copy(src, dst, sem)


def _dispatch_kernel(dest_sm, h_ref, xs_ref, sem):
    i = pl.program_id(0)
    tm = h_ref.shape[0]

    def issue(r, _):
        for k in range(TOP_K):
            d = dest_sm[(i * tm + r) * TOP_K + k]
            _row_copy(h_ref.at[pl.ds(r, 1), :], xs_ref.at[pl.ds(d, 1), :], sem).start()
        return 0

    lax.fori_loop(0, tm, issue, 0)
    for _ in range(TOP_K):
        _row_copy(h_ref, xs_ref.at[pl.ds(0, tm), :], sem).wait()


def _dispatch(dest_flat, h, n_rows):
    t, d = h.shape
    tm = min(t, TOK_TILE)
    return pl.pallas_call(
        _dispatch_kernel,
        grid_spec=pltpu.PrefetchScalarGridSpec(
            num_scalar_prefetch=1,
            grid=(t // tm,),
            in_specs=[pl.BlockSpec((tm, d), lambda i, dd: (i, 0))],
            out_specs=pl.BlockSpec(memory_space=pl.ANY),
            scratch_shapes=[pltpu.SemaphoreType.DMA],
        ),
        out_shape=jax.ShapeDtypeStruct((n_rows, d), F32),
        compiler_params=_cparams(("arbitrary",)),
        name="dispatch",
    )(dest_flat, h)


TAIL_PIECES = tuple(MOE_SUB >> (i + 1) for i in range(MOE_SUB.bit_length() - 1))


def _moe_kernel(ue_sm, ur_sm, ub_sm, na_sm, xs_ref, w1_ref, b1_ref, w2_ref, b2_ref, sel_ref, ys_ref,
                xbuf, acc, w1b, w2b, xsem, ysem):
    del ue_sm
    u = pl.program_id(0)
    f = pl.program_id(1)
    last_f = pl.num_programs(1) - 1
    n_act = na_sm[0]
    d = xbuf.shape[1]
    tf2 = w1_ref.shape[1]

    @pl.when(jnp.logical_and(u == 0, f == 0))
    def _():
        xbuf[...] = jnp.zeros(xbuf.shape, F32)

    def x_copy(base, off, n, slot):
        return pltpu.make_async_copy(xs_ref.at[pl.ds(base + off, n), :], xbuf.at[pl.ds(off, n), :], xsem.at[slot])

    def y_copy(base, bi):
        r0 = pl.multiple_of(bi * MOE_SUB, MOE_SUB)
        return pltpu.make_async_copy(acc.at[pl.ds(r0, MOE_SUB), :], ys_ref.at[pl.ds(base + r0, MOE_SUB), :],
                                     ysem.at[bi])

    def tail_pieces(base, n_full, rem, fn):
        off = n_full * MOE_SUB
        for p in TAIL_PIECES:
            take = (rem & p) != 0

            @pl.when(take)
            def _(off=off, p=p):
                fn(x_copy(base, pl.multiple_of(off, p), p, n_full))

            off = off + jnp.where(take, p, 0)

    @pl.when(u < n_act)
    def _():
        rows = ur_sm[u]
        base = pl.multiple_of(ub_sm[u], MOE_SUB)
        n_full = rows // MOE_SUB
        rem = rows % MOE_SUB
        n_sub = (rows + MOE_SUB - 1) // MOE_SUB

        @pl.when(f == 0)
        def _():
            @pl.when(u > 0)
            def _():
                prev_base = pl.multiple_of(ub_sm[u - 1], MOE_SUB)
                prev_sub = (ur_sm[u - 1] + MOE_SUB - 1) // MOE_SUB

                def drain(bi, _):
                    y_copy(prev_base, bi).wait()
                    return 0

                lax.fori_loop(0, prev_sub, drain, 0)

            def fetch(bi, _):
                x_copy(base, pl.multiple_of(bi * MOE_SUB, MOE_SUB), MOE_SUB, bi).start()
                return 0

            lax.fori_loop(0, n_full, fetch, 0)
            tail_pieces(base, n_full, rem, lambda c: c.start())

        wrows = 256

        def cast_w1(i, _):
            r = pl.multiple_of(i * wrows, wrows)
            w1b[pl.ds(r, wrows), :] = w1_ref[pl.ds(r, wrows), :].astype(BF16)
            return 0

        def cast_w2(i, _):
            r = pl.multiple_of(i * wrows, wrows)
            w2b[pl.ds(r, wrows), :] = w2_ref[pl.ds(r, wrows), :].astype(BF16)
            return 0

        lax.fori_loop(0, w1_ref.shape[0] // wrows, cast_w1, 0)
        lax.fori_loop(0, w2_ref.shape[0] // wrows, cast_w2, 0)

        def sub_block(bi, _):
            r0 = pl.multiple_of(bi * MOE_SUB, MOE_SUB)

            @pl.when(f == 0)
            def _():
                @pl.when(bi < n_full)
                def _():
                    x_copy(base, r0, MOE_SUB, bi).wait()

                @pl.when(bi >= n_full)
                def _():
                    tail_pieces(base, n_full, rem, lambda c: c.wait())

                acc[pl.ds(r0, MOE_SUB), :] = jnp.broadcast_to(b2_ref[...], (MOE_SUB, d))

            xb = xbuf[pl.ds(r0, MOE_SUB), :].astype(BF16)
            uu = jnp.dot(xb, w1b[...], preferred_element_type=F32) + b1_ref[...]
            glu = jnp.minimum(uu, SWIGLU_LIMIT)
            glu = glu * _sigmoid(SWIGLU_ALPHA * glu)
            lin = jnp.clip(uu, -SWIGLU_LIMIT, SWIGLU_LIMIT) + 1.0
            acts = []
            for cidx in range(tf2 // (2 * LANES)):
                cs = slice(cidx * 2 * LANES, (cidx + 1) * 2 * LANES)
                prod = glu[:, cs] * pltpu.roll(lin[:, cs], 2 * LANES - 1, axis=1)
                acts.append(jnp.dot(prod.astype(BF16), sel_ref[...], preferred_element_type=F32))
            act = jnp.concatenate(acts, axis=1).astype(BF16)
            acc[pl.ds(r0, MOE_SUB), :] += jnp.dot(act, w2b[...], preferred_element_type=F32)

            @pl.when(f == last_f)
            def _():
                y_copy(base, bi).start()

            return 0

        lax.fori_loop(0, n_sub, sub_block, 0)

        @pl.when(jnp.logical_and(f == last_f, u == n_act - 1))
        def _():
            def drain(bi, _):
                y_copy(base, bi).wait()
                return 0

            lax.fori_loop(0, n_sub, drain, 0)


def _moe(xs, w1, b1, w2, b2, layer, unit_expert, unit_rows, unit_base, n_active):
    n_rows, d = xs.shape
    n_layers, n_exp, _, ff2 = w1.shape
    ff = ff2 // 2
    n_units = unit_expert.shape[0]
    nf = ff // MOE_TF
    n_slots = MOE_UNIT // MOE_SUB
    sel = (jnp.arange(2 * LANES)[:, None] == 2 * jnp.arange(LANES)[None, :]).astype(BF16)

    def fstep(u, f, na):
        return jnp.where(u < na[0], f, nf - 1)

    return pl.pallas_call(
        _moe_kernel,
        grid_spec=pltpu.PrefetchScalarGridSpec(
            num_scalar_prefetch=4,
            grid=(n_units, nf),
            in_specs=[
                pl.BlockSpec(memory_space=pl.ANY),
                pl.BlockSpec((None, None, d, 2 * MOE_TF),
                             lambda u, f, ue, ur, ub, na: (layer, ue[u], 0, fstep(u, f, na))),
                pl.BlockSpec((None, None, 1, 2 * MOE_TF),
                             lambda u, f, ue, ur, ub, na: (layer, ue[u], 0, fstep(u, f, na))),
                pl.BlockSpec((None, None, MOE_TF, d),
                             lambda u, f, ue, ur, ub, na: (layer, ue[u], fstep(u, f, na), 0)),
                pl.BlockSpec((None, None, 1, d), lambda u, f, ue, ur, ub, na: (layer, ue[u], 0, 0)),
                pl.BlockSpec((2 * LANES, LANES), lambda u, f, ue, ur, ub, na: (0, 0)),
            ],
            out_specs=pl.BlockSpec(memory_space=pl.ANY),
            scratch_shapes=[pltpu.VMEM((MOE_UNIT, d), F32), pltpu.VMEM((MOE_UNIT, d), F32),
                            pltpu.VMEM((d, 2 * MOE_TF), BF16), pltpu.VMEM((MOE_TF, d), BF16),
                            pltpu.SemaphoreType.DMA((n_slots,)), pltpu.SemaphoreType.DMA((n_slots,))],
        ),
        out_shape=jax.ShapeDtypeStruct((n_rows, d), F32),
        compiler_params=_cparams(("arbitrary", "arbitrary")),
        name="moe_experts",
    )(unit_expert, unit_rows, unit_base, n_active, xs, w1, b1.reshape(n_layers, n_exp, 1, ff2), w2,
      b2.reshape(n_layers, n_exp, 1, d), sel)


def _combine_kernel(dest_sm, ys_ref, tg_ref, x1_ref, g2_ref, fg_ref, o_ref, buf, sem, *, final_norm):
    i = pl.program_id(0)
    tm = x1_ref.shape[0]

    def issue(r, _):
        for k in range(TOP_K):
            d = dest_sm[(i * tm + r) * TOP_K + k]
            _row_copy(ys_ref.at[pl.ds(d, 1), :], buf.at[k, pl.ds(r, 1), :], sem).start()
        return 0

    lax.fori_loop(0, tm, issue, 0)
    for k in range(TOP_K):
        _row_copy(ys_ref.at[pl.ds(0, tm), :], buf.at[k], sem).wait()

    rows = 32

    def body(si, _):
        r0 = pl.multiple_of(si * rows, rows)
        tg = tg_ref[pl.ds(r0, rows), :]
        y = tg[:, 0:1] * buf[0, pl.ds(r0, rows), :]
        for k in range(1, TOP_K):
            y = y + tg[:, k:k + 1] * buf[k, pl.ds(r0, rows), :]
        x2 = x1_ref[pl.ds(r0, rows), :] + g2_ref[...] * y
        if final_norm:
            ms = jnp.mean(x2 * x2, axis=-1, keepdims=True)
            x2 = x2 * lax.rsqrt(ms + NORM_EPS) * fg_ref[...]
        o_ref[pl.ds(r0, rows), :] = x2
        return 0

    lax.fori_loop(0, tm // rows, body, 0)


def _combine(dest_flat, ys, tgate, x1, g2, final_g, final_norm):
    t, d = x1.shape
    tm = min(t, TOK_TILE)
    vec = pl.BlockSpec((1, d), lambda i, dd: (0, 0))
    return pl.pallas_call(
        functools.partial(_combine_kernel, final_norm=final_norm),
        grid_spec=pltpu.PrefetchScalarGridSpec(
            num_scalar_prefetch=1,
            grid=(t // tm,),
            in_specs=[pl.BlockSpec(memory_space=pl.ANY),
                      pl.BlockSpec((tm, LANES), lambda i, dd: (i, 0)),
                      pl.BlockSpec((tm, d), lambda i, dd: (i, 0)), vec, vec],
            out_specs=pl.BlockSpec((tm, d), lambda i, dd: (i, 0)),
            scratch_shapes=[pltpu.VMEM((TOP_K, tm, d), F32), pltpu.SemaphoreType.DMA],
        ),
        out_shape=jax.ShapeDtypeStruct((t, d), F32),
        compiler_params=_cparams(("arbitrary",)),
        name="moe_combine",
    )(dest_flat, ys, tgate, x1, g2, final_g)


def _unit_tables(counts, n_units):
    n_exp = counts.shape[0]
    padded = (counts + MOE_SUB - 1) // MOE_SUB * MOE_SUB
    gstart = jnp.cumsum(padded) - padded
    units_per = (counts + MOE_UNIT - 1) // MOE_UNIT
    ends = jnp.cumsum(units_per)
    starts = ends - units_per
    n_active = ends[-1]
    u = jnp.arange(n_units, dtype=jnp.int32)
    uc = jnp.minimum(u, n_active - 1)
    ue = jnp.minimum(jnp.sum((ends[None, :] <= uc[:, None]).astype(jnp.int32), axis=1), n_exp - 1)
    part = uc - starts[ue]
    rows = jnp.where(u < n_active, jnp.clip(counts[ue] - part * MOE_UNIT, 0, MOE_UNIT), 0)
    base = gstart[ue] + part * MOE_UNIT
    return ue.astype(jnp.int32), rows.astype(jnp.int32), base.astype(jnp.int32), n_active.reshape(1).astype(jnp.int32)


def kernel(x, c, positions, ada_w, ada_b, norm1_g, norm2_g, w_in, conv_w, conv_b, lru_wa, lru_ba, lru_wx,
           lru_bx, lru_lambda, attn_out_g, lru_out_g, w_out, router_w, router_b, w1, b1, w2, b2, final_g):
    bsz, seq, d = x.shape
    assert bsz == 1
    t = seq
    n_layers = ada_w.shape[0]
    attn_width = attn_out_g.shape[1]
    lru_width = lru_out_g.shape[1]
    n_experts = router_w.shape[2]
    assert attn_width == lru_width and w_in.shape[2] == 3 * attn_width + 2 * lru_width
    n_units = (t * TOP_K) // MOE_UNIT + n_experts
    n_rows = t * TOP_K + n_experts * MOE_SUB

    xf = x.reshape(t, d)
    pos_col = positions.reshape(t, 1)
    mod = _adaln(c, ada_w, ada_b)
    cos, sin = _rope_tables(pos_col)
    fg = final_g.reshape(1, d)

    for l in range(n_layers):
        sh1, sc1, g1, sh2, sc2, g2 = [mod[l, :, i * d:(i + 1) * d] for i in range(6)]
        z = _inproj(xf, norm1_g[l].reshape(1, d), sc1, sh1, w_in[l].astype(BF16), cos, sin, attn_width)
        attn = _attention(z, attn_width)
        wgate = jnp.concatenate([lru_wa[l], lru_wx[l]], axis=-1).astype(BF16)
        lru_n = _lru(z, pos_col, conv_w[l], conv_b[l].reshape(1, -1), wgate, lru_ba[l].reshape(1, -1),
                     lru_bx[l].reshape(1, -1), lru_lambda[l].reshape(1, -1), lru_out_g[l].reshape(1, -1),
                     3 * attn_width // lru_width)
        wo = w_out[l].astype(BF16)
        rw_pad = jnp.pad(router_w[l], ((0, 0), (0, LANES - n_experts)))
        rw_hi = rw_pad.astype(BF16)
        rw_split = jnp.concatenate([rw_hi, (rw_pad - rw_hi.astype(F32)).astype(BF16)], axis=1)
        rb_pad = jnp.pad(router_b[l], (0, LANES - n_experts)).reshape(1, LANES)
        x1, h, topi, tgate = _outproj(attn, attn_out_g[l].reshape(1, attn_width), lru_n, wo[:attn_width],
                                       wo[attn_width:], xf, g1,
                                       norm2_g[l].reshape(1, d), sc2, sh2, rw_split, rb_pad, n_experts)
        dest, cnt = _route(topi)
        counts = cnt[0, :n_experts].astype(jnp.int32)
        unit_expert, unit_rows, unit_base, n_active = _unit_tables(counts, n_units)
        dest_flat = dest[:, :TOP_K].reshape(t * TOP_K)
        xs = _dispatch(dest_flat, h, n_rows)
        ys = _moe(xs, w1, b1, w2, b2, l, unit_expert, unit_rows, unit_base, n_active)
        xf = _combine(dest_flat, ys, tgate, x1, g2, fg, l == n_layers - 1)
    return xf.reshape(bsz, seq, d)
```

```python
import functools

import jax
import jax.numpy as jnp
from jax import lax
from jax.experimental import pallas as pl
from jax.experimental.pallas import tpu as pltpu

F32 = jnp.float32
BF16 = jnp.bfloat16

HEAD_DIM = 128
LRU_BLOCK_W = 128
CONV_WIDTH = 4
LRU_C = 8.0
ROPE_THETA = 10000.0
DILATED_BRANCHES = ((128, 1), (512, 4), (2048, 16))
Q_BLOCK = 128
NEG_INF = -1e30
NORM_EPS = 1e-6
TOP_K = 4
SWIGLU_LIMIT = 7.0
SWIGLU_ALPHA = 1.702

LANES = 128
SUBLANES = 8
VMEM_LIMIT = 56 * 1024 * 1024

MOE_SUB = 512
MOE_UNIT = 1536
MOE_TF = 512
TOK_TILE = 256


def _cparams(sem, vmem=VMEM_LIMIT):
    return pltpu.CompilerParams(dimension_semantics=sem, vmem_limit_bytes=vmem)


def _sigmoid(x):
    return 0.5 * (1.0 + jnp.tanh(0.5 * x))


def _adaln_kernel(c_ref, w_ref, b_ref, o_ref, cond):
    d, tn = w_ref.shape
    groups = 4
    rows = groups * SUBLANES
    tiles = tn // LANES

    @pl.when(jnp.logical_and(pl.program_id(0) == 0, pl.program_id(1) == 0))
    def _():
        cv = c_ref[...]
        cond[...] = jnp.broadcast_to(cv * _sigmoid(cv), cond.shape)

    def body(i, accs):
        r = pl.multiple_of(i * rows, rows)
        out = []
        for g in range(groups):
            rg = r + g * SUBLANES
            cv = cond[pl.ds(rg, SUBLANES), :]
            for ti in range(tiles):
                out.append(accs[g * tiles + ti] + w_ref[pl.ds(rg, SUBLANES), ti * LANES:(ti + 1) * LANES] * cv)
        return tuple(out)

    zero = jnp.zeros((SUBLANES, LANES), F32)
    accs = lax.fori_loop(0, d // rows, body, tuple(zero for _ in range(groups * tiles)))
    for ti in range(tiles):
        acc = (accs[ti] + accs[tiles + ti]) + (accs[2 * tiles + ti] + accs[3 * tiles + ti])
        cols = slice(ti * LANES, (ti + 1) * LANES)
        o_ref[:, cols] = jnp.sum(acc, axis=0, keepdims=True) + b_ref[:, cols]


def _adaln(c, ada_w, ada_b):
    n_layers, d, n = ada_w.shape
    tn = 1024
    return pl.pallas_call(
        _adaln_kernel,
        grid=(n_layers, n // tn),
        in_specs=[pl.BlockSpec((d, 1), lambda l, j: (0, 0)),
                  pl.BlockSpec((None, d, tn), lambda l, j: (l, 0, j)),
                  pl.BlockSpec((None, 1, tn), lambda l, j: (l, 0, j))],
        out_specs=pl.BlockSpec((None, 1, tn), lambda l, j: (l, 0, j)),
        out_shape=jax.ShapeDtypeStruct((n_layers, 1, n), F32),
        scratch_shapes=[pltpu.VMEM((d, LANES), F32)],
        compiler_params=_cparams(("arbitrary", "arbitrary")),
        name="adaln",
    )(c.reshape(d, 1), ada_w, ada_b.reshape(n_layers, 1, n))


def _rope_kernel(pos_ref, invf_ref, cos_ref, sin_ref):
    ang = pos_ref[...].astype(F32) * invf_ref[...]
    cos_ref[...] = jnp.cos(ang)
    s = jnp.sin(ang)
    lane = lax.broadcasted_iota(jnp.int32, s.shape, 1)
    sin_ref[...] = jnp.where(lane < HEAD_DIM // 2, -s, s)


def _rope_tables(pos_col):
    t = pos_col.shape[0]
    tm = min(t, 1024)
    inv = ROPE_THETA ** (-jnp.arange(0, HEAD_DIM, 2, dtype=F32) / HEAD_DIM)
    inv = jnp.concatenate([inv, inv]).reshape(1, HEAD_DIM)
    return pl.pallas_call(
        _rope_kernel,
        grid=(t // tm,),
        in_specs=[pl.BlockSpec((tm, 1), lambda i: (i, 0)),
                  pl.BlockSpec((1, HEAD_DIM), lambda i: (0, 0))],
        out_specs=[pl.BlockSpec((tm, HEAD_DIM), lambda i: (i, 0))] * 2,
        out_shape=[jax.ShapeDtypeStruct((t, HEAD_DIM), F32)] * 2,
        compiler_params=_cparams(("arbitrary",)),
        name="rope_tables",
    )(pos_col, inv)


def _inproj_kernel(x_ref, g_ref, sc_ref, sh_ref, w_ref, cos_ref, sin_ref, o_ref, h_scr,
                   *, q_tiles, rope_tiles):
    j = pl.program_id(1)
    tm, tn = o_ref.shape
    rows = 32

    @pl.when(j == 0)
    def _():
        a = g_ref[...] * (1.0 + sc_ref[...])
        b = sh_ref[...]

        def body(i, _):
            r = pl.multiple_of(i * rows, rows)
            xv = x_ref[pl.ds(r, rows), :]
            ms = jnp.mean(xv * xv, axis=-1, keepdims=True)
            h_scr[pl.ds(r, rows), :] = (xv * lax.rsqrt(ms + NORM_EPS) * a + b).astype(BF16)
            return 0

        lax.fori_loop(0, tm // rows, body, 0)

    @pl.when(j < rope_tiles)
    def _():
        acc = jnp.dot(h_scr[...], w_ref[...], preferred_element_type=F32)
        scale = jnp.where(j < q_tiles, HEAD_DIM ** -0.5, 1.0).astype(F32)
        cs = cos_ref[...] * scale
        sn = sin_ref[...] * scale
        for c in range(tn // HEAD_DIM):
            cols = slice(c * HEAD_DIM, (c + 1) * HEAD_DIM)
            t = acc[:, cols]
            o_ref[:, cols] = (t * cs + pltpu.roll(t, HEAD_DIM // 2, axis=1) * sn).astype(BF16)

    @pl.when(j >= rope_tiles)
    def _():
        o_ref[...] = jnp.dot(h_scr[...], w_ref[...], preferred_element_type=F32).astype(BF16)


def _inproj(x, g, sc, sh, w_bf16, cos, sin, attn_width):
    t, d = x.shape
    n = w_bf16.shape[1]
    tm, tn = min(t, 1024), 512
    kern = functools.partial(_inproj_kernel, q_tiles=attn_width // tn, rope_tiles=2 * attn_width // tn)
    vec = pl.BlockSpec((1, d), lambda i, j: (0, 0))
    return pl.pallas_call(
        kern,
        grid=(t // tm, n // tn),
        in_specs=[pl.BlockSpec((tm, d), lambda i, j: (i, 0)), vec, vec, vec,
                  pl.BlockSpec((d, tn), lambda i, j: (0, j)),
                  pl.BlockSpec((tm, HEAD_DIM), lambda i, j: (i, 0)),
                  pl.BlockSpec((tm, HEAD_DIM), lambda i, j: (i, 0))],
        out_specs=pl.BlockSpec((tm, tn), lambda i, j: (i, j)),
        out_shape=jax.ShapeDtypeStruct((t, n), BF16),
        scratch_shapes=[pltpu.VMEM((tm, d), BF16)],
        compiler_params=_cparams(("arbitrary", "arbitrary")),
        name="inproj",
    )(x, g, sc, sh, w_bf16, cos, sin)


ATT_TILE = 2048
ATT_HEADS = 2
ATT_UNROLL = 4


def _rows(start, n, stride):
    return pl.ds(start, n) if stride == 1 else pl.ds(start, n, stride=stride)


def _attn_kernel(q_ref, kc_ref, kp_ref, vc_ref, vp_ref, o_ref, qf, kf, vf, oacc, lacc, *, dilations):
    m = pl.program_id(1)
    tp, wcols = q_ref.shape
    nh = wcols // HEAD_DIM
    w = Q_BLOCK
    n_blk = tp // w

    chunk = 256

    def widen(i, _):
        r = pl.multiple_of(i * chunk, chunk)
        for h in range(nh):
            cols = slice(h * HEAD_DIM, (h + 1) * HEAD_DIM)
            qf[h, pl.ds(r, chunk), :] = q_ref[pl.ds(r, chunk), cols].astype(F32)
            kf[h, pl.ds(r, chunk), :] = kp_ref[pl.ds(r, chunk), cols].astype(F32)
            kf[h, pl.ds(tp + r, chunk), :] = kc_ref[pl.ds(r, chunk), cols].astype(F32)
            vf[h, pl.ds(r, chunk), :] = vp_ref[pl.ds(r, chunk), cols].astype(F32)
            vf[h, pl.ds(tp + r, chunk), :] = vc_ref[pl.ds(r, chunk), cols].astype(F32)
        return 0

    lax.fori_loop(0, tp // chunk, widen, 0)

    qi = lax.broadcasted_iota(jnp.int32, (w, 2 * w), 0)
    kk = lax.broadcasted_iota(jnp.int32, (w, 2 * w), 1)
    band = jnp.logical_and(kk >= qi, kk <= qi + w)
    behind = kk >= w

    def block(bi, d, start, padded):
        start_k = start + tp - w * d
        for h in range(nh):
            q = qf[h, _rows(start, w, d), :].astype(BF16)
            k = kf[h, _rows(start_k, 2 * w, d), :].astype(BF16)
            v = vf[h, _rows(start_k, 2 * w, d), :].astype(BF16)
            s = lax.dot_general(q, k, (((1,), (1,)), ((), ())), preferred_element_type=F32)
            valid = jnp.logical_and(band, jnp.logical_or(behind, jnp.logical_not(padded)))
            s = jnp.where(valid, s, NEG_INF)
            mx = jnp.max(s, axis=-1, keepdims=True)
            p = jnp.exp(s - mx)
            den = jnp.sum(p, axis=-1, keepdims=True)
            o = jnp.dot(p.astype(BF16), v, preferred_element_type=F32) / den
            oacc[bi * nh + h, _rows(start, w, d), :] = o
            lacc[bi * nh + h, _rows(start, w, d), :] = jnp.broadcast_to(mx + jnp.log(den), (w, LANES))

    for bi, d in enumerate(dilations):
        per_class = n_blk // d

        def group(it, _, bi=bi, d=d, per_class=per_class):
            for j in range(ATT_UNROLL):
                idx = it * ATT_UNROLL + j
                r = idx // per_class
                nq = idx % per_class
                start = nq * (d * w) + r
                if d == 1:
                    start = pl.multiple_of(start, w)
                block(bi, d, start, jnp.logical_and(m == 0, nq == 0))
            return 0

        lax.fori_loop(0, n_blk // ATT_UNROLL, group, 0)

    rows = 64

    def mix(i, _):
        r = pl.multiple_of(i * rows, rows)
        for h in range(nh):
            cols = slice(h * HEAD_DIM, (h + 1) * HEAD_DIM)
            ls = [lacc[bi * nh + h, pl.ds(r, rows), :] for bi in range(len(dilations))]
            mx = functools.reduce(jnp.maximum, ls)
            es = [jnp.exp(l - mx) for l in ls]
            inv = 1.0 / functools.reduce(jnp.add, es)
            out = es[0] * inv * oacc[h, pl.ds(r, rows), :]
            for bi in range(1, len(dilations)):
                out = out + es[bi] * inv * oacc[bi * nh + h, pl.ds(r, rows), :]
            o_ref[pl.ds(r, rows), cols] = out.astype(o_ref.dtype)
        return 0

    lax.fori_loop(0, tp // rows, mix, 0)


def _attention(z, attn_width):
    t = z.shape[0]
    tp = ATT_TILE
    dilations = tuple(d for _, d in DILATED_BRANCHES)
    assert all(win // d == Q_BLOCK and win <= tp for win, d in DILATED_BRANCHES) and t % tp == 0
    wcols = ATT_HEADS * HEAD_DIM
    ng = attn_width // wcols

    def cur(off):
        return pl.BlockSpec((tp, wcols), lambda g, m: (m, off * ng + g))

    def prev(off):
        return pl.BlockSpec((tp, wcols), lambda g, m: (jnp.maximum(m - 1, 0), off * ng + g))

    nb = len(dilations)
    return pl.pallas_call(
        functools.partial(_attn_kernel, dilations=dilations),
        grid=(ng, t // tp),
        in_specs=[cur(0), cur(1), prev(1), cur(2), prev(2)],
        out_specs=pl.BlockSpec((tp, wcols), lambda g, m: (m, g)),
        out_shape=jax.ShapeDtypeStruct((t, attn_width), BF16),
        scratch_shapes=[pltpu.VMEM((ATT_HEADS, tp, HEAD_DIM), F32), pltpu.VMEM((ATT_HEADS, 2 * tp, HEAD_DIM), F32),
                        pltpu.VMEM((ATT_HEADS, 2 * tp, HEAD_DIM), F32),
                        pltpu.VMEM((nb * ATT_HEADS, tp, HEAD_DIM), F32),
                        pltpu.VMEM((nb * ATT_HEADS, tp, LANES), F32)],
        compiler_params=_cparams(("arbitrary", "arbitrary")),
        name="attention",
    )(z, z, z, z, z)


def _softplus(x):
    return jnp.maximum(x, 0.0) + jnp.log1p(jnp.exp(-jnp.abs(x)))


def _gelu_tanh(x):
    return 0.5 * x * (1.0 + jnp.tanh(0.7978845608028654 * (x + 0.044715 * x * x * x)))


def _lru_kernel(xr_ref, gr_ref, pos_ref, cw_ref, cb_ref, wg_ref, ba_ref, bx_ref, lam_ref, g_ref,
                o_ref, xbuf, a_scr, b_scr, hcar, *, n_blocks):
    i = pl.program_id(0)
    tc, width = a_scr.shape
    pad = SUBLANES
    bw = LRU_BLOCK_W

    @pl.when(i == 0)
    def _():
        xbuf[0:pad, :] = jnp.zeros((pad, width), F32)
        hcar[...] = jnp.zeros(hcar.shape, F32)

    xbuf[pad:, :] = xr_ref[...].astype(F32)
    reset = pos_ref[...] == 0
    sub = lax.broadcasted_iota(jnp.int32, (tc, bw), 0) % SUBLANES

    for hb in range(n_blocks):
        cols = slice(hb * bw, (hb + 1) * bw)
        xc = cb_ref[:, cols] + cw_ref[0:1, cols] * xbuf[pad - 3:pad - 3 + tc, cols]
        for k in range(1, CONV_WIDTH):
            xc = xc + cw_ref[k:k + 1, cols] * xbuf[pad - 3 + k:pad - 3 + k + tc, cols]
        gates = jnp.dot(xc.astype(BF16), wg_ref[hb], preferred_element_type=F32)
        r = _sigmoid(gates[:, :bw] + ba_ref[:, cols])
        ig = _sigmoid(gates[:, bw:] + bx_ref[:, cols])
        log_a = -LRU_C * r * _softplus(-lam_ref[:, cols])
        ea = jnp.exp(log_a)
        a = jnp.where(reset, 0.0, ea)
        mult = jnp.where(reset, 1.0, jnp.sqrt(-jnp.tanh(log_a) * (ea * ea + 1.0)))
        b = xc * ig * mult
        for s in (1, 2, 4):
            a_s = pltpu.roll(a, s, axis=0)
            b_s = pltpu.roll(b, s, axis=0)
            keep = sub >= s
            b = jnp.where(keep, a * b_s + b, b)
            a = jnp.where(keep, a * a_s, a)
        a_scr[:, cols] = a
        b_scr[:, cols] = b

    def group(gi, h_in):
        r0 = pl.multiple_of(gi * SUBLANES, SUBLANES)
        hh = a_scr[pl.ds(r0, SUBLANES), :] * h_in + b_scr[pl.ds(r0, SUBLANES), :]
        a_scr[pl.ds(r0, SUBLANES), :] = hh
        return jnp.broadcast_to(hh[SUBLANES - 1:SUBLANES, :], hh.shape)

    hcar[...] = lax.fori_loop(0, tc // SUBLANES, group, hcar[...])
    xbuf[0:pad, :] = xbuf[tc:tc + pad, :]

    rows = 32

    def epilogue(si, _):
        r0 = pl.multiple_of(si * rows, rows)
        y = a_scr[pl.ds(r0, rows), :] * _gelu_tanh(gr_ref[pl.ds(r0, rows), :].astype(F32))
        ms = jnp.mean(y * y, axis=-1, keepdims=True)
        o_ref[pl.ds(r0, rows), :] = (y * lax.rsqrt(ms + NORM_EPS) * g_ref[...]).astype(BF16)
        return 0

    lax.fori_loop(0, tc // rows, epilogue, 0)


def _lru(z, pos_col, conv_w, conv_b, wgate_bf16, ba, bx, lam, g, col_block):
    t = z.shape[0]
    width = conv_w.shape[1]
    tc = min(t, 128)
    n_blocks = width // LRU_BLOCK_W
    vec = pl.BlockSpec((1, width), lambda i: (0, 0))
    return pl.pallas_call(
        functools.partial(_lru_kernel, n_blocks=n_blocks),
        grid=(t // tc,),
        in_specs=[pl.BlockSpec((tc, width), lambda i: (i, col_block)),
                  pl.BlockSpec((tc, width), lambda i: (i, col_block + 1)),
                  pl.BlockSpec((tc, 1), lambda i: (i, 0)),
                  pl.BlockSpec((CONV_WIDTH, width), lambda i: (0, 0)), vec,
                  pl.BlockSpec((n_blocks, LRU_BLOCK_W, 2 * LRU_BLOCK_W), lambda i: (0, 0, 0)),
                  vec, vec, vec, vec],
        out_specs=pl.BlockSpec((tc, width), lambda i: (i, 0)),
        out_shape=jax.ShapeDtypeStruct((t, width), BF16),
        scratch_shapes=[pltpu.VMEM((tc + SUBLANES, width), F32), pltpu.VMEM((tc, width), F32),
                        pltpu.VMEM((tc, width), F32), pltpu.VMEM((SUBLANES, width), F32)],
        compiler_params=_cparams(("arbitrary",)),
        name="rg_lru",
    )(z, z, pos_col, conv_w, conv_b, wgate_bf16, ba, bx, lam, g)


def _outproj_kernel(a_ref, ag_ref, r_ref, wa_ref, wr_ref, x_ref, g1_ref, n2_ref, sc_ref, sh_ref, rw_ref, rb_ref,
                    x1_ref, h_ref, ti_ref, tg_ref, *, n_experts):
    a = a_ref[...].astype(F32)
    a = a * lax.rsqrt(jnp.mean(a * a, axis=-1, keepdims=True) + NORM_EPS) * ag_ref[...]
    y = jnp.dot(a.astype(BF16), wa_ref[...], preferred_element_type=F32)
    y = y + jnp.dot(r_ref[...], wr_ref[...], preferred_element_type=F32)
    x1 = x_ref[...] + g1_ref[...] * y
    x1_ref[...] = x1
    ms = jnp.mean(x1 * x1, axis=-1, keepdims=True)
    h = x1 * lax.rsqrt(ms + NORM_EPS) * (n2_ref[...] * (1.0 + sc_ref[...])) + sh_ref[...]
    h_ref[...] = h
    h_hi = h.astype(BF16)
    h_lo = (h - h_hi.astype(F32)).astype(BF16)
    p1 = jnp.dot(h_hi, rw_ref[...], preferred_element_type=F32)
    p2 = jnp.dot(h_lo, rw_ref[:, 0:LANES], preferred_element_type=F32)
    logits = p1[:, 0:LANES] + p1[:, LANES:] + p2 + rb_ref[...]
    lane = lax.broadcasted_iota(jnp.int32, logits.shape, 1).astype(F32)
    cur = jnp.where(lane < n_experts, logits, -jnp.inf)
    vals, idxs = [], []
    for _ in range(TOP_K):
        m = jnp.max(cur, axis=-1, keepdims=True)
        idx = jnp.min(jnp.where(cur == m, lane, float(LANES)), axis=-1, keepdims=True)
        vals.append(m)
        idxs.append(idx)
        cur = jnp.where(lane == idx, -jnp.inf, cur)
    es = [jnp.exp(v - vals[0]) for v in vals]
    inv = 1.0 / (es[0] + es[1] + es[2] + es[3])
    ti = jnp.zeros(logits.shape, F32)
    tg = jnp.zeros(logits.shape, F32)
    for k in range(TOP_K):
        ti = jnp.where(lane == k, idxs[k], ti)
        tg = jnp.where(lane == k, es[k] * inv, tg)
    ti_ref[...] = ti
    tg_ref[...] = tg


def _outproj(attn, attn_g, lru_n, wa_bf16, wr_bf16, x, g1, n2, sc, sh, rw_split, rb_pad, n_experts):
    t, d = x.shape
    wa = attn.shape[1]
    wr = lru_n.shape[1]
    tm = min(t, 512)
    vec = pl.BlockSpec((1, d), lambda i: (0, 0))
    lanes = pl.BlockSpec((tm, LANES), lambda i: (i, 0))
    return pl.pallas_call(
        functools.partial(_outproj_kernel, n_experts=n_experts),
        grid=(t // tm,),
        in_specs=[pl.BlockSpec((tm, wa), lambda i: (i, 0)), pl.BlockSpec((1, wa), lambda i: (0, 0)),
                  pl.BlockSpec((tm, wr), lambda i: (i, 0)),
                  pl.BlockSpec((wa, d), lambda i: (0, 0)), pl.BlockSpec((wr, d), lambda i: (0, 0)),
                  pl.BlockSpec((tm, d), lambda i: (i, 0)), vec, vec, vec, vec,
                  pl.BlockSpec((d, 2 * LANES), lambda i: (0, 0)), pl.BlockSpec((1, LANES), lambda i: (0, 0))],
        out_specs=[pl.BlockSpec((tm, d), lambda i: (i, 0)), pl.BlockSpec((tm, d), lambda i: (i, 0)),
                   lanes, lanes],
        out_shape=[jax.ShapeDtypeStruct((t, d), F32), jax.ShapeDtypeStruct((t, d), F32),
                   jax.ShapeDtypeStruct((t, LANES), F32), jax.ShapeDtypeStruct((t, LANES), F32)],
        compiler_params=_cparams(("arbitrary",)),
        name="outproj",
    )(attn, attn_g, lru_n, wa_bf16, wr_bf16, x, g1, n2, sc, sh, rw_split, rb_pad)


def _route_kernel(ti_ref, dest_ref, cnt_ref, carry, gstart):
    ph = pl.program_id(0)
    i = pl.program_id(1)
    last = pl.num_programs(1) - 1
    tm = ti_ref.shape[0]
    ti = ti_ref[...]
    lane = lax.broadcasted_iota(jnp.int32, (tm, LANES), 1).astype(F32)
    sel = [lane == ti[:, k:k + 1] for k in range(TOP_K)]
    hot = jnp.zeros((tm, LANES), F32)
    for k in range(TOP_K):
        hot = jnp.where(sel[k], 1.0, hot)

    @pl.when(jnp.logical_and(ph == 0, i == 0))
    def _():
        carry[...] = jnp.zeros(carry.shape, F32)

    @pl.when(ph == 1)
    def _():
        row = lax.broadcasted_iota(jnp.int32, (tm, tm), 0)
        col = lax.broadcasted_iota(jnp.int32, (tm, tm), 1)
        tri = jnp.where(col < row, 1.0, 0.0).astype(BF16)
        before = jnp.dot(tri, hot.astype(BF16), preferred_element_type=F32) + carry[0:1, :]
        base = before + gstart[0:1, :]
        dest = jnp.zeros((tm, LANES), F32)
        for k in range(TOP_K):
            dk = jnp.sum(jnp.where(sel[k], base, 0.0), axis=-1, keepdims=True)
            dest = jnp.where(lane == k, dk, dest)
        dest_ref[...] = dest.astype(jnp.int32)

    carry[...] = carry[...] + jnp.sum(hot, axis=0, keepdims=True)

    @pl.when(jnp.logical_and(ph == 0, i == last))
    def _():
        cnt = carry[...]
        cnt_ref[...] = cnt[0:1, :]
        padded = jnp.ceil(cnt / MOE_SUB) * MOE_SUB
        lane8 = lax.broadcasted_iota(jnp.int32, cnt.shape, 1)
        acc = padded
        for s in (1, 2, 4, 8, 16, 32, 64):
            acc = acc + jnp.where(lane8 >= s, pltpu.roll(acc, s, axis=1), 0.0)
        gstart[...] = acc - padded
        carry[...] = jnp.zeros(carry.shape, F32)


def _route(topi):
    t = topi.shape[0]
    tm = min(t, TOK_TILE)
    dest, cnt = pl.pallas_call(
        _route_kernel,
        grid=(2, t // tm),
        in_specs=[pl.BlockSpec((tm, LANES), lambda p, i: (i, 0))],
        out_specs=[pl.BlockSpec((tm, LANES), lambda p, i: (i * p, 0)),
                   pl.BlockSpec((1, LANES), lambda p, i: (0, 0))],
        out_shape=[jax.ShapeDtypeStruct((t, LANES), jnp.int32), jax.ShapeDtypeStruct((1, LANES), F32)],
        scratch_shapes=[pltpu.VMEM((SUBLANES, LANES), F32), pltpu.VMEM((SUBLANES, LANES), F32)],
        compiler_params=_cparams(("arbitrary", "arbitrary")),
        name="route",
    )(topi)
    return dest, cnt


def _row_copy(src, dst, sem):
    return pltpu.make_async_copy(src, dst, sem)


def _dispatch_kernel(dest_sm, h_ref, xs_ref, sem):
    i = pl.program_id(0)
    tm = h_ref.shape[0]

    def issue(r, _):
        for k in range(TOP_K):
            d = dest_sm[(i * tm + r) * TOP_K + k]
            _row_copy(h_ref.at[pl.ds(r, 1), :], xs_ref.at[pl.ds(d, 1), :], sem).start()
        return 0

    lax.fori_loop(0, tm, issue, 0)
    for _ in range(TOP_K):
        _row_copy(h_ref, xs_ref.at[pl.ds(0, tm), :], sem).wait()


def _dispatch(dest_flat, h, n_rows):
    t, d = h.shape
    tm = min(t, TOK_TILE)
    return pl.pallas_call(
        _dispatch_kernel,
        grid_spec=pltpu.PrefetchScalarGridSpec(
            num_scalar_prefetch=1,
            grid=(t // tm,),
            in_specs=[pl.BlockSpec((tm, d), lambda i, dd: (i, 0))],
            out_specs=pl.BlockSpec(memory_space=pl.ANY),
            scratch_shapes=[pltpu.SemaphoreType.DMA],
        ),
        out_shape=jax.ShapeDtypeStruct((n_rows, d), F32),
        compiler_params=_cparams(("arbitrary",)),
        name="dispatch",
    )(dest_flat, h)


TAIL_PIECES = tuple(MOE_SUB >> (i + 1) for i in range(MOE_SUB.bit_length() - 1))


def _moe_kernel(ue_sm, ur_sm, ub_sm, na_sm, xs_ref, w1_ref, b1_ref, w2_ref, b2_ref, sel_ref, ys_ref,
                xbuf, acc, xsem, ysem):
    del ue_sm
    u = pl.program_id(0)
    f = pl.program_id(1)
    last_f = pl.num_programs(1) - 1
    n_act = na_sm[0]
    d = xbuf.shape[1]
    tf2 = w1_ref.shape[1]

    @pl.when(jnp.logical_and(u == 0, f == 0))
    def _():
        xbuf[...] = jnp.zeros(xbuf.shape, F32)

    def x_copy(base, off, n, slot):
        return pltpu.make_async_copy(xs_ref.at[pl.ds(base + off, n), :], xbuf.at[pl.ds(off, n), :], xsem.at[slot])

    def y_copy(base, bi):
        r0 = pl.multiple_of(bi * MOE_SUB, MOE_SUB)
        return pltpu.make_async_copy(acc.at[pl.ds(r0, MOE_SUB), :], ys_ref.at[pl.ds(base + r0, MOE_SUB), :],
                                     ysem.at[bi])

    def tail_pieces(base, n_full, rem, fn):
        off = n_full * MOE_SUB
        for p in TAIL_PIECES:
            take = (rem & p) != 0

            @pl.when(take)
            def _(off=off, p=p):
                fn(x_copy(base, pl.multiple_of(off, p), p, n_full))

            off = off + jnp.where(take, p, 0)

    @pl.when(u < n_act)
    def _():
        rows = ur_sm[u]
        base = pl.multiple_of(ub_sm[u], MOE_SUB)
        n_full = rows // MOE_SUB
        rem = rows % MOE_SUB
        n_sub = (rows + MOE_SUB - 1) // MOE_SUB

        @pl.when(f == 0)
        def _():
            @pl.when(u > 0)
            def _():
                prev_base = pl.multiple_of(ub_sm[u - 1], MOE_SUB)
                prev_sub = (ur_sm[u - 1] + MOE_SUB - 1) // MOE_SUB

                def drain(bi, _):
                    y_copy(prev_base, bi).wait()
                    return 0

                lax.fori_loop(0, prev_sub, drain, 0)

            def fetch(bi, _):
                x_copy(base, pl.multiple_of(bi * MOE_SUB, MOE_SUB), MOE_SUB, bi).start()
                return 0

            lax.fori_loop(0, n_full, fetch, 0)
            tail_pieces(base, n_full, rem, lambda c: c.start())

        def sub_block(bi, _):
            r0 = pl.multiple_of(bi * MOE_SUB, MOE_SUB)

            @pl.when(f == 0)
            def _():
                @pl.when(bi < n_full)
                def _():
                    x_copy(base, r0, MOE_SUB, bi).wait()

                @pl.when(bi >= n_full)
                def _():
                    tail_pieces(base, n_full, rem, lambda c: c.wait())

                acc[pl.ds(r0, MOE_SUB), :] = jnp.broadcast_to(b2_ref[...], (MOE_SUB, d))

            xb = xbuf[pl.ds(r0, MOE_SUB), :].astype(BF16)
            uu = jnp.dot(xb, w1_ref[...].astype(BF16), preferred_element_type=F32) + b1_ref[...]
            glu = jnp.minimum(uu, SWIGLU_LIMIT)
            glu = glu * _sigmoid(SWIGLU_ALPHA * glu)
            lin = jnp.clip(uu, -SWIGLU_LIMIT, SWIGLU_LIMIT) + 1.0
            acts = []
            for cidx in range(tf2 // (2 * LANES)):
                cs = slice(cidx * 2 * LANES, (cidx + 1) * 2 * LANES)
                prod = glu[:, cs] * pltpu.roll(lin[:, cs], 2 * LANES - 1, axis=1)
                acts.append(jnp.dot(prod.astype(BF16), sel_ref[...], preferred_element_type=F32))
            act = jnp.concatenate(acts, axis=1).astype(BF16)
            acc[pl.ds(r0, MOE_SUB), :] += jnp.dot(act, w2_ref[...].astype(BF16), preferred_element_type=F32)

            @pl.when(f == last_f)
            def _():
                y_copy(base, bi).start()

            return 0

        lax.fori_loop(0, n_sub, sub_block, 0)

        @pl.when(jnp.logical_and(f == last_f, u == n_act - 1))
        def _():
            def drain(bi, _):
                y_copy(base, bi).wait()
                return 0

            lax.fori_loop(0, n_sub, drain, 0)


def _moe(xs, w1, b1, w2, b2, layer, unit_expert, unit_rows, unit_base, n_active):
    n_rows, d = xs.shape
    n_layers, n_exp, _, ff2 = w1.shape
    ff = ff2 // 2
    n_units = unit_expert.shape[0]
    nf = ff // MOE_TF
    n_slots = MOE_UNIT // MOE_SUB
    sel = (jnp.arange(2 * LANES)[:, None] == 2 * jnp.arange(LANES)[None, :]).astype(BF16)

    def fstep(u, f, na):
        return jnp.where(u < na[0], f, nf - 1)

    return pl.pallas_call(
        _moe_kernel,
        grid_spec=pltpu.PrefetchScalarGridSpec(
            num_scalar_prefetch=4,
            grid=(n_units, nf),
            in_specs=[
                pl.BlockSpec(memory_space=pl.ANY),
                pl.BlockSpec((None, None, d, 2 * MOE_TF),
                             lambda u, f, ue, ur, ub, na: (layer, ue[u], 0, fstep(u, f, na))),
                pl.BlockSpec((None, None, 1, 2 * MOE_TF),
                             lambda u, f, ue, ur, ub, na: (layer, ue[u], 0, fstep(u, f, na))),
                pl.BlockSpec((None, None, MOE_TF, d),
                             lambda u, f, ue, ur, ub, na: (layer, ue[u], fstep(u, f, na), 0)),
                pl.BlockSpec((None, None, 1, d), lambda u, f, ue, ur, ub, na: (layer, ue[u], 0, 0)),
                pl.BlockSpec((2 * LANES, LANES), lambda u, f, ue, ur, ub, na: (0, 0)),
            ],
            out_specs=pl.BlockSpec(memory_space=pl.ANY),
            scratch_shapes=[pltpu.VMEM((MOE_UNIT, d), F32), pltpu.VMEM((MOE_UNIT, d), F32),
                            pltpu.SemaphoreType.DMA((n_slots,)), pltpu.SemaphoreType.DMA((n_slots,))],
        ),
        out_shape=jax.ShapeDtypeStruct((n_rows, d), F32),
        compiler_params=_cparams(("arbitrary", "arbitrary")),
        name="moe_experts",
    )(unit_expert, unit_rows, unit_base, n_active, xs, w1, b1.reshape(n_layers, n_exp, 1, ff2), w2,
      b2.reshape(n_layers, n_exp, 1, d), sel)


def _combine_kernel(dest_sm, ys_ref, tg_ref, x1_ref, g2_ref, fg_ref, o_ref, buf, sem, *, final_norm):
    i = pl.program_id(0)
    tm = x1_ref.shape[0]

    def issue(r, _):
        for k in range(TOP_K):
            d = dest_sm[(i * tm + r) * TOP_K + k]
            _row_copy(ys_ref.at[pl.ds(d, 1), :], buf.at[k, pl.ds(r, 1), :], sem).start()
        return 0

    lax.fori_loop(0, tm, issue, 0)
    for k in range(TOP_K):
        _row_copy(ys_ref.at[pl.ds(0, tm), :], buf.at[k], sem).wait()

    rows = 32

    def body(si, _):
        r0 = pl.multiple_of(si * rows, rows)
        tg = tg_ref[pl.ds(r0, rows), :]
        y = tg[:, 0:1] * buf[0, pl.ds(r0, rows), :]
        for k in range(1, TOP_K):
            y = y + tg[:, k:k + 1] * buf[k, pl.ds(r0, rows), :]
        x2 = x1_ref[pl.ds(r0, rows), :] + g2_ref[...] * y
        if final_norm:
            ms = jnp.mean(x2 * x2, axis=-1, keepdims=True)
            x2 = x2 * lax.rsqrt(ms + NORM_EPS) * fg_ref[...]
        o_ref[pl.ds(r0, rows), :] = x2
        return 0

    lax.fori_loop(0, tm // rows, body, 0)


def _combine(dest_flat, ys, tgate, x1, g2, final_g, final_norm):
    t, d = x1.shape
    tm = min(t, TOK_TILE)
    vec = pl.BlockSpec((1, d), lambda i, dd: (0, 0))
    return pl.pallas_call(
        functools.partial(_combine_kernel, final_norm=final_norm),
        grid_spec=pltpu.PrefetchScalarGridSpec(
            num_scalar_prefetch=1,
            grid=(t // tm,),
            in_specs=[pl.BlockSpec(memory_space=pl.ANY),
                      pl.BlockSpec((tm, LANES), lambda i, dd: (i, 0)),
                      pl.BlockSpec((tm, d), lambda i, dd: (i, 0)), vec, vec],
            out_specs=pl.BlockSpec((tm, d), lambda i, dd: (i, 0)),
            scratch_shapes=[pltpu.VMEM((TOP_K, tm, d), F32), pltpu.SemaphoreType.DMA],
        ),
        out_shape=jax.ShapeDtypeStruct((t, d), F32),
        compiler_params=_cparams(("arbitrary",)),
        name="moe_combine",
    )(dest_flat, ys, tgate, x1, g2, final_g)


def _unit_tables(counts, n_units):
    n_exp = counts.shape[0]
    padded = (counts + MOE_SUB - 1) // MOE_SUB * MOE_SUB
    gstart = jnp.cumsum(padded) - padded
    units_per = (counts + MOE_UNIT - 1) // MOE_UNIT
    ends = jnp.cumsum(units_per)
    starts = ends - units_per
    n_active = ends[-1]
    u = jnp.arange(n_units, dtype=jnp.int32)
    uc = jnp.minimum(u, n_active - 1)
    ue = jnp.minimum(jnp.sum((ends[None, :] <= uc[:, None]).astype(jnp.int32), axis=1), n_exp - 1)
    part = uc - starts[ue]
    rows = jnp.where(u < n_active, jnp.clip(counts[ue] - part * MOE_UNIT, 0, MOE_UNIT), 0)
    base = gstart[ue] + part * MOE_UNIT
    return ue.astype(jnp.int32), rows.astype(jnp.int32), base.astype(jnp.int32), n_active.reshape(1).astype(jnp.int32)


def kernel(x, c, positions, ada_w, ada_b, norm1_g, norm2_g, w_in, conv_w, conv_b, lru_wa, lru_ba, lru_wx,
           lru_bx, lru_lambda, attn_out_g, lru_out_g, w_out, router_w, router_b, w1, b1, w2, b2, final_g):
    bsz, seq, d = x.shape
    assert bsz == 1
    t = seq
    n_layers = ada_w.shape[0]
    attn_width = attn_out_g.shape[1]
    lru_width = lru_out_g.shape[1]
    n_experts = router_w.shape[2]
    assert attn_width == lru_width and w_in.shape[2] == 3 * attn_width + 2 * lru_width
    n_units = (t * TOP_K) // MOE_UNIT + n_experts
    n_rows = t * TOP_K + n_experts * MOE_SUB

    xf = x.reshape(t, d)
    pos_col = positions.reshape(t, 1)
    mod = _adaln(c, ada_w, ada_b)
    cos, sin = _rope_tables(pos_col)
    fg = final_g.reshape(1, d)

    for l in range(n_layers):
        sh1, sc1, g1, sh2, sc2, g2 = [mod[l, :, i * d:(i + 1) * d] for i in range(6)]
        z = _inproj(xf, norm1_g[l].reshape(1, d), sc1, sh1, w_in[l].astype(BF16), cos, sin, attn_width)
        attn = _attention(z, attn_width)
        wgate = jnp.concatenate([lru_wa[l], lru_wx[l]], axis=-1).astype(BF16)
        lru_n = _lru(z, pos_col, conv_w[l], conv_b[l].reshape(1, -1), wgate, lru_ba[l].reshape(1, -1),
                     lru_bx[l].reshape(1, -1), lru_lambda[l].reshape(1, -1), lru_out_g[l].reshape(1, -1),
                     3 * attn_width // lru_width)
        wo = w_out[l].astype(BF16)
        rw_pad = jnp.pad(router_w[l], ((0, 0), (0, LANES - n_experts)))
        rw_hi = rw_pad.astype(BF16)
        rw_split = jnp.concatenate([rw_hi, (rw_pad - rw_hi.astype(F32)).astype(BF16)], axis=1)
        rb_pad = jnp.pad(router_b[l], (0, LANES - n_experts)).reshape(1, LANES)
        x1, h, topi, tgate = _outproj(attn, attn_out_g[l].reshape(1, attn_width), lru_n, wo[:attn_width],
                                       wo[attn_width:], xf, g1,
                                       norm2_g[l].reshape(1, d), sc2, sh2, rw_split, rb_pad, n_experts)
        dest, cnt = _route(topi)
        counts = cnt[0, :n_experts].astype(jnp.int32)
        unit_expert, unit_rows, unit_base, n_active = _unit_tables(counts, n_units)
        dest_flat = dest[:, :TOP_K].reshape(t * TOP_K)
        xs = _dispatch(dest_flat, h, n_rows)
        ys = _moe(xs, w1, b1, w2, b2, l, unit_expert, unit_rows, unit_base, n_active)
        xf = _combine(dest_flat, ys, tgate, x1, g2, fg, l == n_layers - 1)
    return xf.reshape(bsz, seq, d)
```

```python
import functools

import jax
import jax.numpy as jnp
from jax import lax
from jax.experimental import pallas as pl
from jax.experimental.pallas import tpu as pltpu

F32 = jnp.float32
BF16 = jnp.bfloat16

HEAD_DIM = 128
LRU_BLOCK_W = 128
CONV_WIDTH = 4
LRU_C = 8.0
ROPE_THETA = 10000.0
DILATED_BRANCHES = ((128, 1), (512, 4), (2048, 16))
Q_BLOCK = 128
NEG_INF = -1e30
NORM_EPS = 1e-6
TOP_K = 4
SWIGLU_LIMIT = 7.0
SWIGLU_ALPHA = 1.702

LANES = 128
SUBLANES = 8
VMEM_LIMIT = 56 * 1024 * 1024

MOE_SUB = 256
MOE_PAIR = 2 * MOE_SUB
MOE_UNIT = 1536
MOE_TF = 512
TOK_TILE = 256


def _cparams(sem, vmem=VMEM_LIMIT):
    return pltpu.CompilerParams(dimension_semantics=sem, vmem_limit_bytes=vmem)


def _sigmoid(x):
    return 0.5 * (1.0 + jnp.tanh(0.5 * x))


def _adaln_kernel(c_ref, w_ref, b_ref, o_ref, cond):
    d, tn = w_ref.shape
    groups = 4
    rows = groups * SUBLANES
    tiles = tn // LANES

    @pl.when(jnp.logical_and(pl.program_id(0) == 0, pl.program_id(1) == 0))
    def _():
        cv = c_ref[...]
        cond[...] = jnp.broadcast_to(cv * _sigmoid(cv), cond.shape)

    def body(i, accs):
        r = pl.multiple_of(i * rows, rows)
        out = []
        for g in range(groups):
            rg = r + g * SUBLANES
            cv = cond[pl.ds(rg, SUBLANES), :]
            for ti in range(tiles):
                out.append(accs[g * tiles + ti] + w_ref[pl.ds(rg, SUBLANES), ti * LANES:(ti + 1) * LANES] * cv)
        return tuple(out)

    zero = jnp.zeros((SUBLANES, LANES), F32)
    accs = lax.fori_loop(0, d // rows, body, tuple(zero for _ in range(groups * tiles)))
    for ti in range(tiles):
        acc = (accs[ti] + accs[tiles + ti]) + (accs[2 * tiles + ti] + accs[3 * tiles + ti])
        cols = slice(ti * LANES, (ti + 1) * LANES)
        o_ref[:, cols] = jnp.sum(acc, axis=0, keepdims=True) + b_ref[:, cols]


def _adaln(c, ada_w, ada_b):
    n_layers, d, n = ada_w.shape
    tn = 1024
    return pl.pallas_call(
        _adaln_kernel,
        grid=(n_layers, n // tn),
        in_specs=[pl.BlockSpec((d, 1), lambda l, j: (0, 0)),
                  pl.BlockSpec((None, d, tn), lambda l, j: (l, 0, j)),
                  pl.BlockSpec((None, 1, tn), lambda l, j: (l, 0, j))],
        out_specs=pl.BlockSpec((None, 1, tn), lambda l, j: (l, 0, j)),
        out_shape=jax.ShapeDtypeStruct((n_layers, 1, n), F32),
        scratch_shapes=[pltpu.VMEM((d, LANES), F32)],
        compiler_params=_cparams(("arbitrary", "arbitrary")),
        name="adaln",
    )(c.reshape(d, 1), ada_w, ada_b.reshape(n_layers, 1, n))


def _rope_kernel(pos_ref, invf_ref, cos_ref, sin_ref):
    ang = pos_ref[...].astype(F32) * invf_ref[...]
    cos_ref[...] = jnp.cos(ang)
    s = jnp.sin(ang)
    lane = lax.broadcasted_iota(jnp.int32, s.shape, 1)
    sin_ref[...] = jnp.where(lane < HEAD_DIM // 2, -s, s)


def _rope_tables(pos_col):
    t = pos_col.shape[0]
    tm = min(t, 1024)
    inv = ROPE_THETA ** (-jnp.arange(0, HEAD_DIM, 2, dtype=F32) / HEAD_DIM)
    inv = jnp.concatenate([inv, inv]).reshape(1, HEAD_DIM)
    return pl.pallas_call(
        _rope_kernel,
        grid=(t // tm,),
        in_specs=[pl.BlockSpec((tm, 1), lambda i: (i, 0)),
                  pl.BlockSpec((1, HEAD_DIM), lambda i: (0, 0))],
        out_specs=[pl.BlockSpec((tm, HEAD_DIM), lambda i: (i, 0))] * 2,
        out_shape=[jax.ShapeDtypeStruct((t, HEAD_DIM), F32)] * 2,
        compiler_params=_cparams(("arbitrary",)),
        name="rope_tables",
    )(pos_col, inv)


def _inproj_kernel(x_ref, g_ref, sc_ref, sh_ref, w_ref, cos_ref, sin_ref, o_ref, h_scr,
                   *, q_tiles, rope_tiles):
    j = pl.program_id(1)
    tm, tn = o_ref.shape
    rows = 32

    @pl.when(j == 0)
    def _():
        a = g_ref[...] * (1.0 + sc_ref[...])
        b = sh_ref[...]

        def body(i, _):
            r = pl.multiple_of(i * rows, rows)
            xv = x_ref[pl.ds(r, rows), :]
            ms = jnp.mean(xv * xv, axis=-1, keepdims=True)
            h_scr[pl.ds(r, rows), :] = (xv * lax.rsqrt(ms + NORM_EPS) * a + b).astype(BF16)
            return 0

        lax.fori_loop(0, tm // rows, body, 0)

    @pl.when(j < rope_tiles)
    def _():
        acc = jnp.dot(h_scr[...], w_ref[...], preferred_element_type=F32)
        scale = jnp.where(j < q_tiles, HEAD_DIM ** -0.5, 1.0).astype(F32)
        cs = cos_ref[...] * scale
        sn = sin_ref[...] * scale
        for c in range(tn // HEAD_DIM):
            cols = slice(c * HEAD_DIM, (c + 1) * HEAD_DIM)
            t = acc[:, cols]
            o_ref[:, cols] = (t * cs + pltpu.roll(t, HEAD_DIM // 2, axis=1) * sn).astype(BF16)

    @pl.when(j >= rope_tiles)
    def _():
        o_ref[...] = jnp.dot(h_scr[...], w_ref[...], preferred_element_type=F32).astype(BF16)


def _inproj(x, g, sc, sh, w_bf16, cos, sin, attn_width):
    t, d = x.shape
    n = w_bf16.shape[1]
    tm, tn = min(t, 1024), 512
    kern = functools.partial(_inproj_kernel, q_tiles=attn_width // tn, rope_tiles=2 * attn_width // tn)
    vec = pl.BlockSpec((1, d), lambda i, j: (0, 0))
    return pl.pallas_call(
        kern,
        grid=(t // tm, n // tn),
        in_specs=[pl.BlockSpec((tm, d), lambda i, j: (i, 0)), vec, vec, vec,
                  pl.BlockSpec((d, tn), lambda i, j: (0, j)),
                  pl.BlockSpec((tm, HEAD_DIM), lambda i, j: (i, 0)),
                  pl.BlockSpec((tm, HEAD_DIM), lambda i, j: (i, 0))],
        out_specs=pl.BlockSpec((tm, tn), lambda i, j: (i, j)),
        out_shape=jax.ShapeDtypeStruct((t, n), BF16),
        scratch_shapes=[pltpu.VMEM((tm, d), BF16)],
        compiler_params=_cparams(("arbitrary", "arbitrary")),
        name="inproj",
    )(x, g, sc, sh, w_bf16, cos, sin)


ATT_TILE = 2048
ATT_HEADS = 2
ATT_UNROLL = 4


def _rows(start, n, stride):
    return pl.ds(start, n) if stride == 1 else pl.ds(start, n, stride=stride)


def _attn_kernel(q_ref, kc_ref, kp_ref, vc_ref, vp_ref, o_ref, qf, kf, vf, oacc, lacc, *, dilations):
    m = pl.program_id(1)
    tp, wcols = q_ref.shape
    nh = wcols // HEAD_DIM
    w = Q_BLOCK
    n_blk = tp // w

    chunk = 256

    def widen(i, _):
        r = pl.multiple_of(i * chunk, chunk)
        for h in range(nh):
            cols = slice(h * HEAD_DIM, (h + 1) * HEAD_DIM)
            qf[h, pl.ds(r, chunk), :] = q_ref[pl.ds(r, chunk), cols].astype(F32)
            kf[h, pl.ds(r, chunk), :] = kp_ref[pl.ds(r, chunk), cols].astype(F32)
            kf[h, pl.ds(tp + r, chunk), :] = kc_ref[pl.ds(r, chunk), cols].astype(F32)
            vf[h, pl.ds(r, chunk), :] = vp_ref[pl.ds(r, chunk), cols].astype(F32)
            vf[h, pl.ds(tp + r, chunk), :] = vc_ref[pl.ds(r, chunk), cols].astype(F32)
        return 0

    lax.fori_loop(0, tp // chunk, widen, 0)

    qi = lax.broadcasted_iota(jnp.int32, (w, 2 * w), 0)
    kk = lax.broadcasted_iota(jnp.int32, (w, 2 * w), 1)
    band = jnp.logical_and(kk >= qi, kk <= qi + w)
    behind = kk >= w

    def block(bi, d, start, padded):
        start_k = start + tp - w * d
        for h in range(nh):
            q = qf[h, _rows(start, w, d), :].astype(BF16)
            k = kf[h, _rows(start_k, 2 * w, d), :].astype(BF16)
            v = vf[h, _rows(start_k, 2 * w, d), :].astype(BF16)
            s = lax.dot_general(q, k, (((1,), (1,)), ((), ())), preferred_element_type=F32)
            valid = jnp.logical_and(band, jnp.logical_or(behind, jnp.logical_not(padded)))
            s = jnp.where(valid, s, NEG_INF)
            mx = jnp.max(s, axis=-1, keepdims=True)
            p = jnp.exp(s - mx)
            den = jnp.sum(p, axis=-1, keepdims=True)
            o = jnp.dot(p.astype(BF16), v, preferred_element_type=F32) / den
            oacc[bi * nh + h, _rows(start, w, d), :] = o
            lacc[bi * nh + h, _rows(start, w, d), :] = jnp.broadcast_to(mx + jnp.log(den), (w, LANES))

    for bi, d in enumerate(dilations):
        per_class = n_blk // d

        def group(it, _, bi=bi, d=d, per_class=per_class):
            for j in range(ATT_UNROLL):
                idx = it * ATT_UNROLL + j
                r = idx // per_class
                nq = idx % per_class
                start = nq * (d * w) + r
                if d == 1:
                    start = pl.multiple_of(start, w)
                block(bi, d, start, jnp.logical_and(m == 0, nq == 0))
            return 0

        lax.fori_loop(0, n_blk // ATT_UNROLL, group, 0)

    rows = 64

    def mix(i, _):
        r = pl.multiple_of(i * rows, rows)
        for h in range(nh):
            cols = slice(h * HEAD_DIM, (h + 1) * HEAD_DIM)
            ls = [lacc[bi * nh + h, pl.ds(r, rows), :] for bi in range(len(dilations))]
            mx = functools.reduce(jnp.maximum, ls)
            es = [jnp.exp(l - mx) for l in ls]
            inv = 1.0 / functools.reduce(jnp.add, es)
            out = es[0] * inv * oacc[h, pl.ds(r, rows), :]
            for bi in range(1, len(dilations)):
                out = out + es[bi] * inv * oacc[bi * nh + h, pl.ds(r, rows), :]
            o_ref[pl.ds(r, rows), cols] = out.astype(o_ref.dtype)
        return 0

    lax.fori_loop(0, tp // rows, mix, 0)


def _attention(z, attn_width):
    t = z.shape[0]
    tp = ATT_TILE
    dilations = tuple(d for _, d in DILATED_BRANCHES)
    assert all(win // d == Q_BLOCK and win <= tp for win, d in DILATED_BRANCHES) and t % tp == 0
    wcols = ATT_HEADS * HEAD_DIM
    ng = attn_width // wcols

    def cur(off):
        return pl.BlockSpec((tp, wcols), lambda g, m: (m, off * ng + g))

    def prev(off):
        return pl.BlockSpec((tp, wcols), lambda g, m: (jnp.maximum(m - 1, 0), off * ng + g))

    nb = len(dilations)
    return pl.pallas_call(
        functools.partial(_attn_kernel, dilations=dilations),
        grid=(ng, t // tp),
        in_specs=[cur(0), cur(1), prev(1), cur(2), prev(2)],
        out_specs=pl.BlockSpec((tp, wcols), lambda g, m: (m, g)),
        out_shape=jax.ShapeDtypeStruct((t, attn_width), BF16),
        scratch_shapes=[pltpu.VMEM((ATT_HEADS, tp, HEAD_DIM), F32), pltpu.VMEM((ATT_HEADS, 2 * tp, HEAD_DIM), F32),
                        pltpu.VMEM((ATT_HEADS, 2 * tp, HEAD_DIM), F32),
                        pltpu.VMEM((nb * ATT_HEADS, tp, HEAD_DIM), F32),
                        pltpu.VMEM((nb * ATT_HEADS, tp, LANES), F32)],
        compiler_params=_cparams(("arbitrary", "arbitrary")),
        name="attention",
    )(z, z, z, z, z)


def _softplus(x):
    return jnp.maximum(x, 0.0) + jnp.log1p(jnp.exp(-jnp.abs(x)))


def _gelu_tanh(x):
    return 0.5 * x * (1.0 + jnp.tanh(0.7978845608028654 * (x + 0.044715 * x * x * x)))


def _lru_kernel(xr_ref, gr_ref, pos_ref, cw_ref, cb_ref, wg_ref, ba_ref, bx_ref, lam_ref, g_ref,
                o_ref, xbuf, a_scr, b_scr, hcar, *, n_blocks):
    i = pl.program_id(0)
    tc, width = a_scr.shape
    pad = SUBLANES
    bw = LRU_BLOCK_W

    @pl.when(i == 0)
    def _():
        xbuf[0:pad, :] = jnp.zeros((pad, width), F32)
        hcar[...] = jnp.zeros(hcar.shape, F32)

    xbuf[pad:, :] = xr_ref[...].astype(F32)
    reset = pos_ref[...] == 0
    sub = lax.broadcasted_iota(jnp.int32, (tc, bw), 0) % SUBLANES

    for hb in range(n_blocks):
        cols = slice(hb * bw, (hb + 1) * bw)
        xc = cb_ref[:, cols] + cw_ref[0:1, cols] * xbuf[pad - 3:pad - 3 + tc, cols]
        for k in range(1, CONV_WIDTH):
            xc = xc + cw_ref[k:k + 1, cols] * xbuf[pad - 3 + k:pad - 3 + k + tc, cols]
        gates = jnp.dot(xc.astype(BF16), wg_ref[hb], preferred_element_type=F32)
        r = _sigmoid(gates[:, :bw] + ba_ref[:, cols])
        ig = _sigmoid(gates[:, bw:] + bx_ref[:, cols])
        log_a = -LRU_C * r * _softplus(-lam_ref[:, cols])
        ea = jnp.exp(log_a)
        a = jnp.where(reset, 0.0, ea)
        mult = jnp.where(reset, 1.0, jnp.sqrt(-jnp.tanh(log_a) * (ea * ea + 1.0)))
        b = xc * ig * mult
        for s in (1, 2, 4):
            a_s = pltpu.roll(a, s, axis=0)
            b_s = pltpu.roll(b, s, axis=0)
            keep = sub >= s
            b = jnp.where(keep, a * b_s + b, b)
            a = jnp.where(keep, a * a_s, a)
        a_scr[:, cols] = a
        b_scr[:, cols] = b

    def group(gi, h_in):
        r0 = pl.multiple_of(gi * SUBLANES, SUBLANES)
        hh = a_scr[pl.ds(r0, SUBLANES), :] * h_in + b_scr[pl.ds(r0, SUBLANES), :]
        a_scr[pl.ds(r0, SUBLANES), :] = hh
        return jnp.broadcast_to(hh[SUBLANES - 1:SUBLANES, :], hh.shape)

    hcar[...] = lax.fori_loop(0, tc // SUBLANES, group, hcar[...])
    xbuf[0:pad, :] = xbuf[tc:tc + pad, :]

    rows = 32

    def epilogue(si, _):
        r0 = pl.multiple_of(si * rows, rows)
        y = a_scr[pl.ds(r0, rows), :] * _gelu_tanh(gr_ref[pl.ds(r0, rows), :].astype(F32))
        ms = jnp.mean(y * y, axis=-1, keepdims=True)
        o_ref[pl.ds(r0, rows), :] = (y * lax.rsqrt(ms + NORM_EPS) * g_ref[...]).astype(BF16)
        return 0

    lax.fori_loop(0, tc // rows, epilogue, 0)


def _lru(z, pos_col, conv_w, conv_b, wgate_bf16, ba, bx, lam, g, col_block):
    t = z.shape[0]
    width = conv_w.shape[1]
    tc = min(t, 128)
    n_blocks = width // LRU_BLOCK_W
    vec = pl.BlockSpec((1, width), lambda i: (0, 0))
    return pl.pallas_call(
        functools.partial(_lru_kernel, n_blocks=n_blocks),
        grid=(t // tc,),
        in_specs=[pl.BlockSpec((tc, width), lambda i: (i, col_block)),
                  pl.BlockSpec((tc, width), lambda i: (i, col_block + 1)),
                  pl.BlockSpec((tc, 1), lambda i: (i, 0)),
                  pl.BlockSpec((CONV_WIDTH, width), lambda i: (0, 0)), vec,
                  pl.BlockSpec((n_blocks, LRU_BLOCK_W, 2 * LRU_BLOCK_W), lambda i: (0, 0, 0)),
                  vec, vec, vec, vec],
        out_specs=pl.BlockSpec((tc, width), lambda i: (i, 0)),
        out_shape=jax.ShapeDtypeStruct((t, width), BF16),
        scratch_shapes=[pltpu.VMEM((tc + SUBLANES, width), F32), pltpu.VMEM((tc, width), F32),
                        pltpu.VMEM((tc, width), F32), pltpu.VMEM((SUBLANES, width), F32)],
        compiler_params=_cparams(("arbitrary",)),
        name="rg_lru",
    )(z, z, pos_col, conv_w, conv_b, wgate_bf16, ba, bx, lam, g)


def _outproj_kernel(a_ref, ag_ref, r_ref, wa_ref, wr_ref, x_ref, g1_ref, n2_ref, sc_ref, sh_ref, rw_ref, rb_ref,
                    x1_ref, h_ref, ti_ref, tg_ref, *, n_experts):
    a = a_ref[...].astype(F32)
    a = a * lax.rsqrt(jnp.mean(a * a, axis=-1, keepdims=True) + NORM_EPS) * ag_ref[...]
    y = jnp.dot(a.astype(BF16), wa_ref[...], preferred_element_type=F32)
    y = y + jnp.dot(r_ref[...], wr_ref[...], preferred_element_type=F32)
    x1 = x_ref[...] + g1_ref[...] * y
    x1_ref[...] = x1
    ms = jnp.mean(x1 * x1, axis=-1, keepdims=True)
    h = x1 * lax.rsqrt(ms + NORM_EPS) * (n2_ref[...] * (1.0 + sc_ref[...])) + sh_ref[...]
    h_ref[...] = h
    h_hi = h.astype(BF16)
    h_lo = (h - h_hi.astype(F32)).astype(BF16)
    p1 = jnp.dot(h_hi, rw_ref[...], preferred_element_type=F32)
    p2 = jnp.dot(h_lo, rw_ref[:, 0:LANES], preferred_element_type=F32)
    logits = p1[:, 0:LANES] + p1[:, LANES:] + p2 + rb_ref[...]
    lane = lax.broadcasted_iota(jnp.int32, logits.shape, 1).astype(F32)
    cur = jnp.where(lane < n_experts, logits, -jnp.inf)
    vals, idxs = [], []
    for _ in range(TOP_K):
        m = jnp.max(cur, axis=-1, keepdims=True)
        idx = jnp.min(jnp.where(cur == m, lane, float(LANES)), axis=-1, keepdims=True)
        vals.append(m)
        idxs.append(idx)
        cur = jnp.where(lane == idx, -jnp.inf, cur)
    es = [jnp.exp(v - vals[0]) for v in vals]
    inv = 1.0 / (es[0] + es[1] + es[2] + es[3])
    ti = jnp.zeros(logits.shape, F32)
    tg = jnp.zeros(logits.shape, F32)
    for k in range(TOP_K):
        ti = jnp.where(lane == k, idxs[k], ti)
        tg = jnp.where(lane == k, es[k] * inv, tg)
    ti_ref[...] = ti
    tg_ref[...] = tg


def _outproj(attn, attn_g, lru_n, wa_bf16, wr_bf16, x, g1, n2, sc, sh, rw_split, rb_pad, n_experts):
    t, d = x.shape
    wa = attn.shape[1]
    wr = lru_n.shape[1]
    tm = min(t, 512)
    vec = pl.BlockSpec((1, d), lambda i: (0, 0))
    lanes = pl.BlockSpec((tm, LANES), lambda i: (i, 0))
    return pl.pallas_call(
        functools.partial(_outproj_kernel, n_experts=n_experts),
        grid=(t // tm,),
        in_specs=[pl.BlockSpec((tm, wa), lambda i: (i, 0)), pl.BlockSpec((1, wa), lambda i: (0, 0)),
                  pl.BlockSpec((tm, wr), lambda i: (i, 0)),
                  pl.BlockSpec((wa, d), lambda i: (0, 0)), pl.BlockSpec((wr, d), lambda i: (0, 0)),
                  pl.BlockSpec((tm, d), lambda i: (i, 0)), vec, vec, vec, vec,
                  pl.BlockSpec((d, 2 * LANES), lambda i: (0, 0)), pl.BlockSpec((1, LANES), lambda i: (0, 0))],
        out_specs=[pl.BlockSpec((tm, d), lambda i: (i, 0)), pl.BlockSpec((tm, d), lambda i: (i, 0)),
                   lanes, lanes],
        out_shape=[jax.ShapeDtypeStruct((t, d), F32), jax.ShapeDtypeStruct((t, d), F32),
                   jax.ShapeDtypeStruct((t, LANES), F32), jax.ShapeDtypeStruct((t, LANES), F32)],
        compiler_params=_cparams(("arbitrary",)),
        name="outproj",
    )(attn, attn_g, lru_n, wa_bf16, wr_bf16, x, g1, n2, sc, sh, rw_split, rb_pad)


def _route_kernel(ti_ref, dest_ref, cnt_ref, carry, gstart):
    ph = pl.program_id(0)
    i = pl.program_id(1)
    last = pl.num_programs(1) - 1
    tm = ti_ref.shape[0]
    ti = ti_ref[...]
    lane = lax.broadcasted_iota(jnp.int32, (tm, LANES), 1).astype(F32)
    sel = [lane == ti[:, k:k + 1] for k in range(TOP_K)]
    hot = jnp.zeros((tm, LANES), F32)
    for k in range(TOP_K):
        hot = jnp.where(sel[k], 1.0, hot)

    @pl.when(jnp.logical_and(ph == 0, i == 0))
    def _():
        carry[...] = jnp.zeros(carry.shape, F32)

    @pl.when(ph == 1)
    def _():
        row = lax.broadcasted_iota(jnp.int32, (tm, tm), 0)
        col = lax.broadcasted_iota(jnp.int32, (tm, tm), 1)
        tri = jnp.where(col < row, 1.0, 0.0).astype(BF16)
        before = jnp.dot(tri, hot.astype(BF16), preferred_element_type=F32) + carry[0:1, :]
        base = before + gstart[0:1, :]
        dest = jnp.zeros((tm, LANES), F32)
        for k in range(TOP_K):
            dk = jnp.sum(jnp.where(sel[k], base, 0.0), axis=-1, keepdims=True)
            dest = jnp.where(lane == k, dk, dest)
        dest_ref[...] = dest.astype(jnp.int32)

    carry[...] = carry[...] + jnp.sum(hot, axis=0, keepdims=True)

    @pl.when(jnp.logical_and(ph == 0, i == last))
    def _():
        cnt = carry[...]
        cnt_ref[...] = cnt[0:1, :]
        padded = jnp.ceil(cnt / MOE_SUB) * MOE_SUB
        lane8 = lax.broadcasted_iota(jnp.int32, cnt.shape, 1)
        acc = padded
        for s in (1, 2, 4, 8, 16, 32, 64):
            acc = acc + jnp.where(lane8 >= s, pltpu.roll(acc, s, axis=1), 0.0)
        gstart[...] = acc - padded
        carry[...] = jnp.zeros(carry.shape, F32)


def _route(topi):
    t = topi.shape[0]
    tm = min(t, TOK_TILE)
    dest, cnt = pl.pallas_call(
        _route_kernel,
        grid=(2, t // tm),
        in_specs=[pl.BlockSpec((tm, LANES), lambda p, i: (i, 0))],
        out_specs=[pl.BlockSpec((tm, LANES), lambda p, i: (i * p, 0)),
                   pl.BlockSpec((1, LANES), lambda p, i: (0, 0))],
        out_shape=[jax.ShapeDtypeStruct((t, LANES), jnp.int32), jax.ShapeDtypeStruct((1, LANES), F32)],
        scratch_shapes=[pltpu.VMEM((SUBLANES, LANES), F32), pltpu.VMEM((SUBLANES, LANES), F32)],
        compiler_params=_cparams(("arbitrary", "arbitrary")),
        name="route",
    )(topi)
    return dest, cnt


def _row_copy(src, dst, sem):
    return pltpu.make_async_copy(src, dst, sem)


def _dispatch_kernel(dest_sm, h_ref, xs_ref, sem):
    i = pl.program_id(0)
    tm = h_ref.shape[0]

    def issue(r, _):
        for k in range(TOP_K):
            d = dest_sm[(i * tm + r) * TOP_K + k]
            _row_copy(h_ref.at[pl.ds(r, 1), :], xs_ref.at[pl.ds(d, 1), :], sem).start()
        return 0

    lax.fori_loop(0, tm, issue, 0)
    for _ in range(TOP_K):
        _row_copy(h_ref, xs_ref.at[pl.ds(0, tm), :], sem).wait()


def _dispatch(dest_flat, h, n_rows):
    t, d = h.shape
    tm = min(t, TOK_TILE)
    return pl.pallas_call(
        _dispatch_kernel,
        grid_spec=pltpu.PrefetchScalarGridSpec(
            num_scalar_prefetch=1,
            grid=(t // tm,),
            in_specs=[pl.BlockSpec((tm, d), lambda i, dd: (i, 0))],
            out_specs=pl.BlockSpec(memory_space=pl.ANY),
            scratch_shapes=[pltpu.SemaphoreType.DMA],
        ),
        out_shape=jax.ShapeDtypeStruct((n_rows, d), F32),
        compiler_params=_cparams(("arbitrary",)),
        name="dispatch",
    )(dest_flat, h)


TAIL_PIECES = tuple(MOE_SUB >> (i + 1) for i in range(MOE_SUB.bit_length() - 1))


def _moe_kernel(ue_sm, ur_sm, ub_sm, na_sm, xs_ref, w1_ref, b1_ref, w2_ref, b2_ref, sel_ref, ys_ref,
                xbuf, acc, xsem, ysem):
    del ue_sm
    u = pl.program_id(0)
    f = pl.program_id(1)
    last_f = pl.num_programs(1) - 1
    n_act = na_sm[0]
    d = xbuf.shape[1]
    tf2 = w1_ref.shape[1]

    @pl.when(jnp.logical_and(u == 0, f == 0))
    def _():
        xbuf[...] = jnp.zeros(xbuf.shape, F32)

    def x_copy(base, off, n, slot):
        return pltpu.make_async_copy(xs_ref.at[pl.ds(base + off, n), :], xbuf.at[pl.ds(off, n), :], xsem.at[slot])

    def y_copy(base, bi):
        r0 = pl.multiple_of(bi * MOE_SUB, MOE_SUB)
        return pltpu.make_async_copy(acc.at[pl.ds(r0, MOE_SUB), :], ys_ref.at[pl.ds(base + r0, MOE_SUB), :],
                                     ysem.at[bi])

    def tail_pieces(base, n_full, rem, fn):
        off = n_full * MOE_SUB
        for p in TAIL_PIECES:
            take = (rem & p) != 0

            @pl.when(take)
            def _(off=off, p=p):
                fn(x_copy(base, pl.multiple_of(off, p), p, n_full))

            off = off + jnp.where(take, p, 0)

    @pl.when(u < n_act)
    def _():
        rows = ur_sm[u]
        base = pl.multiple_of(ub_sm[u], MOE_SUB)
        n_full = rows // MOE_SUB
        rem = rows % MOE_SUB
        n_sub = (rows + MOE_SUB - 1) // MOE_SUB

        @pl.when(f == 0)
        def _():
            @pl.when(u > 0)
            def _():
                prev_base = pl.multiple_of(ub_sm[u - 1], MOE_SUB)
                prev_sub = (ur_sm[u - 1] + MOE_SUB - 1) // MOE_SUB

                def drain(bi, _):
                    y_copy(prev_base, bi).wait()
                    return 0

                lax.fori_loop(0, prev_sub, drain, 0)

            def fetch(bi, _):
                x_copy(base, pl.multiple_of(bi * MOE_SUB, MOE_SUB), MOE_SUB, bi).start()
                return 0

            lax.fori_loop(0, n_full, fetch, 0)
            tail_pieces(base, n_full, rem, lambda c: c.start())

        def wait_x(g):
            @pl.when(g < n_full)
            def _():
                x_copy(base, pl.multiple_of(g * MOE_SUB, MOE_SUB), MOE_SUB, g).wait()

            @pl.when(g >= n_full)
            def _():
                tail_pieces(base, n_full, rem, lambda c: c.wait())

        def compute(r0, m, subs):
            @pl.when(f == 0)
            def _():
                for g in subs:
                    wait_x(g)
                acc[pl.ds(r0, m), :] = jnp.broadcast_to(b2_ref[...], (m, d))

            xb = xbuf[pl.ds(r0, m), :].astype(BF16)
            uu = jnp.dot(xb, w1_ref[...].astype(BF16), preferred_element_type=F32) + b1_ref[...]
            glu = jnp.minimum(uu, SWIGLU_LIMIT)
            glu = glu * _sigmoid(SWIGLU_ALPHA * glu)
            lin = jnp.clip(uu, -SWIGLU_LIMIT, SWIGLU_LIMIT) + 1.0
            acts = []
            for cidx in range(tf2 // (2 * LANES)):
                cs = slice(cidx * 2 * LANES, (cidx + 1) * 2 * LANES)
                prod = glu[:, cs] * pltpu.roll(lin[:, cs], 2 * LANES - 1, axis=1)
                acts.append(jnp.dot(prod.astype(BF16), sel_ref[...], preferred_element_type=F32))
            act = jnp.concatenate(acts, axis=1).astype(BF16)
            acc[pl.ds(r0, m), :] += jnp.dot(act, w2_ref[...].astype(BF16), preferred_element_type=F32)

            @pl.when(f == last_f)
            def _():
                for g in subs:
                    y_copy(base, g).start()

        def pair(bi, _):
            compute(pl.multiple_of(bi * MOE_PAIR, MOE_PAIR), MOE_PAIR, (2 * bi, 2 * bi + 1))
            return 0

        lax.fori_loop(0, n_sub // 2, pair, 0)

        @pl.when(n_sub % 2 == 1)
        def _():
            compute(pl.multiple_of((n_sub - 1) * MOE_SUB, MOE_SUB), MOE_SUB, (n_sub - 1,))

        @pl.when(jnp.logical_and(f == last_f, u == n_act - 1))
        def _():
            def drain(bi, _):
                y_copy(base, bi).wait()
                return 0

            lax.fori_loop(0, n_sub, drain, 0)


def _moe(xs, w1, b1, w2, b2, layer, unit_expert, unit_rows, unit_base, n_active):
    n_rows, d = xs.shape
    n_layers, n_exp, _, ff2 = w1.shape
    ff = ff2 // 2
    n_units = unit_expert.shape[0]
    nf = ff // MOE_TF
    n_slots = MOE_UNIT // MOE_SUB
    sel = (jnp.arange(2 * LANES)[:, None] == 2 * jnp.arange(LANES)[None, :]).astype(BF16)

    def fstep(u, f, na):
        return jnp.where(u < na[0], f, nf - 1)

    return pl.pallas_call(
        _moe_kernel,
        grid_spec=pltpu.PrefetchScalarGridSpec(
            num_scalar_prefetch=4,
            grid=(n_units, nf),
            in_specs=[
                pl.BlockSpec(memory_space=pl.ANY),
                pl.BlockSpec((None, None, d, 2 * MOE_TF),
                             lambda u, f, ue, ur, ub, na: (layer, ue[u], 0, fstep(u, f, na))),
                pl.BlockSpec((None, None, 1, 2 * MOE_TF),
                             lambda u, f, ue, ur, ub, na: (layer, ue[u], 0, fstep(u, f, na))),
                pl.BlockSpec((None, None, MOE_TF, d),
                             lambda u, f, ue, ur, ub, na: (layer, ue[u], fstep(u, f, na), 0)),
                pl.BlockSpec((None, None, 1, d), lambda u, f, ue, ur, ub, na: (layer, ue[u], 0, 0)),
                pl.BlockSpec((2 * LANES, LANES), lambda u, f, ue, ur, ub, na: (0, 0)),
            ],
            out_specs=pl.BlockSpec(memory_space=pl.ANY),
            scratch_shapes=[pltpu.VMEM((MOE_UNIT, d), F32), pltpu.VMEM((MOE_UNIT, d), F32),
                            pltpu.SemaphoreType.DMA((n_slots,)), pltpu.SemaphoreType.DMA((n_slots,))],
        ),
        out_shape=jax.ShapeDtypeStruct((n_rows, d), F32),
        compiler_params=_cparams(("arbitrary", "arbitrary")),
        name="moe_experts",
    )(unit_expert, unit_rows, unit_base, n_active, xs, w1, b1.reshape(n_layers, n_exp, 1, ff2), w2,
      b2.reshape(n_layers, n_exp, 1, d), sel)


def _combine_kernel(dest_sm, ys_ref, tg_ref, x1_ref, g2_ref, fg_ref, o_ref, buf, sem, *, final_norm):
    i = pl.program_id(0)
    tm = x1_ref.shape[0]

    def issue(r, _):
        for k in range(TOP_K):
            d = dest_sm[(i * tm + r) * TOP_K + k]
            _row_copy(ys_ref.at[pl.ds(d, 1), :], buf.at[k, pl.ds(r, 1), :], sem).start()
        return 0

    lax.fori_loop(0, tm, issue, 0)
    for k in range(TOP_K):
        _row_copy(ys_ref.at[pl.ds(0, tm), :], buf.at[k], sem).wait()

    rows = 32

    def body(si, _):
        r0 = pl.multiple_of(si * rows, rows)
        tg = tg_ref[pl.ds(r0, rows), :]
        y = tg[:, 0:1] * buf[0, pl.ds(r0, rows), :]
        for k in range(1, TOP_K):
            y = y + tg[:, k:k + 1] * buf[k, pl.ds(r0, rows), :]
        x2 = x1_ref[pl.ds(r0, rows), :] + g2_ref[...] * y
        if final_norm:
            ms = jnp.mean(x2 * x2, axis=-1, keepdims=True)
            x2 = x2 * lax.rsqrt(ms + NORM_EPS) * fg_ref[...]
        o_ref[pl.ds(r0, rows), :] = x2
        return 0

    lax.fori_loop(0, tm // rows, body, 0)


def _combine(dest_flat, ys, tgate, x1, g2, final_g, final_norm):
    t, d = x1.shape
    tm = min(t, TOK_TILE)
    vec = pl.BlockSpec((1, d), lambda i, dd: (0, 0))
    return pl.pallas_call(
        functools.partial(_combine_kernel, final_norm=final_norm),
        grid_spec=pltpu.PrefetchScalarGridSpec(
            num_scalar_prefetch=1,
            grid=(t // tm,),
            in_specs=[pl.BlockSpec(memory_space=pl.ANY),
                      pl.BlockSpec((tm, LANES), lambda i, dd: (i, 0)),
                      pl.BlockSpec((tm, d), lambda i, dd: (i, 0)), vec, vec],
            out_specs=pl.BlockSpec((tm, d), lambda i, dd: (i, 0)),
            scratch_shapes=[pltpu.VMEM((TOP_K, tm, d), F32), pltpu.SemaphoreType.DMA],
        ),
        out_shape=jax.ShapeDtypeStruct((t, d), F32),
        compiler_params=_cparams(("arbitrary",)),
        name="moe_combine",
    )(dest_flat, ys, tgate, x1, g2, final_g)


def _unit_tables(counts, n_units):
    n_exp = counts.shape[0]
    padded = (counts + MOE_SUB - 1) // MOE_SUB * MOE_SUB
    gstart = jnp.cumsum(padded) - padded
    units_per = (counts + MOE_UNIT - 1) // MOE_UNIT
    ends = jnp.cumsum(units_per)
    starts = ends - units_per
    n_active = ends[-1]
    u = jnp.arange(n_units, dtype=jnp.int32)
    uc = jnp.minimum(u, n_active - 1)
    ue = jnp.minimum(jnp.sum((ends[None, :] <= uc[:, None]).astype(jnp.int32), axis=1), n_exp - 1)
    part = uc - starts[ue]
    rows = jnp.where(u < n_active, jnp.clip(counts[ue] - part * MOE_UNIT, 0, MOE_UNIT), 0)
    base = gstart[ue] + part * MOE_UNIT
    return ue.astype(jnp.int32), rows.astype(jnp.int32), base.astype(jnp.int32), n_active.reshape(1).astype(jnp.int32)


def kernel(x, c, positions, ada_w, ada_b, norm1_g, norm2_g, w_in, conv_w, conv_b, lru_wa, lru_ba, lru_wx,
           lru_bx, lru_lambda, attn_out_g, lru_out_g, w_out, router_w, router_b, w1, b1, w2, b2, final_g):
    bsz, seq, d = x.shape
    assert bsz == 1
    t = seq
    n_layers = ada_w.shape[0]
    attn_width = attn_out_g.shape[1]
    lru_width = lru_out_g.shape[1]
    n_experts = router_w.shape[2]
    assert attn_width == lru_width and w_in.shape[2] == 3 * attn_width + 2 * lru_width
    n_units = (t * TOP_K) // MOE_UNIT + n_experts
    n_rows = t * TOP_K + n_experts * MOE_SUB

    xf = x.reshape(t, d)
    pos_col = positions.reshape(t, 1)
    mod = _adaln(c, ada_w, ada_b)
    cos, sin = _rope_tables(pos_col)
    fg = final_g.reshape(1, d)

    for l in range(n_layers):
        sh1, sc1, g1, sh2, sc2, g2 = [mod[l, :, i * d:(i + 1) * d] for i in range(6)]
        z = _inproj(xf, norm1_g[l].reshape(1, d), sc1, sh1, w_in[l].astype(BF16), cos, sin, attn_width)
        attn = _attention(z, attn_width)
        wgate = jnp.concatenate([lru_wa[l], lru_wx[l]], axis=-1).astype(BF16)
        lru_n = _lru(z, pos_col, conv_w[l], conv_b[l].reshape(1, -1), wgate, lru_ba[l].reshape(1, -1),
                     lru_bx[l].reshape(1, -1), lru_lambda[l].reshape(1, -1), lru_out_g[l].reshape(1, -1),
                     3 * attn_width // lru_width)
        wo = w_out[l].astype(BF16)
        rw_pad = jnp.pad(router_w[l], ((0, 0), (0, LANES - n_experts)))
        rw_hi = rw_pad.astype(BF16)
        rw_split = jnp.concatenate([rw_hi, (rw_pad - rw_hi.astype(F32)).astype(BF16)], axis=1)
        rb_pad = jnp.pad(router_b[l], (0, LANES - n_experts)).reshape(1, LANES)
        x1, h, topi, tgate = _outproj(attn, attn_out_g[l].reshape(1, attn_width), lru_n, wo[:attn_width],
                                       wo[attn_width:], xf, g1,
                                       norm2_g[l].reshape(1, d), sc2, sh2, rw_split, rb_pad, n_experts)
        dest, cnt = _route(topi)
        counts = cnt[0, :n_experts].astype(jnp.int32)
        unit_expert, unit_rows, unit_base, n_active = _unit_tables(counts, n_units)
        dest_flat = dest[:, :TOP_K].reshape(t * TOP_K)
        xs = _dispatch(dest_flat, h, n_rows)
        ys = _moe(xs, w1, b1, w2, b2, l, unit_expert, unit_rows, unit_base, n_active)
        xf = _combine(dest_flat, ys, tgate, x1, g2, fg, l == n_layers - 1)
    return xf.reshape(bsz, seq, d)
```

```python
import functools

import jax
import jax.numpy as jnp
from jax import lax
from jax.experimental import pallas as pl
from jax.experimental.pallas import tpu as pltpu

F32 = jnp.float32
BF16 = jnp.bfloat16

HEAD_DIM = 128
LRU_BLOCK_W = 128
CONV_WIDTH = 4
LRU_C = 8.0
ROPE_THETA = 10000.0
DILATED_BRANCHES = ((128, 1), (512, 4), (2048, 16))
Q_BLOCK = 128
NEG_INF = -1e30
NORM_EPS = 1e-6
TOP_K = 4
SWIGLU_LIMIT = 7.0
SWIGLU_ALPHA = 1.702

LANES = 128
SUBLANES = 8
VMEM_LIMIT = 56 * 1024 * 1024

MOE_SUB = 256
MOE_PAIR = 2 * MOE_SUB
MOE_UNIT = 1536
MOE_TF = 512
TOK_TILE = 256


def _cparams(sem, vmem=VMEM_LIMIT):
    return pltpu.CompilerParams(dimension_semantics=sem, vmem_limit_bytes=vmem)


def _sigmoid(x):
    return 0.5 * (1.0 + jnp.tanh(0.5 * x))


def _adaln_kernel(c_ref, w_ref, b_ref, o_ref, cond):
    d, tn = w_ref.shape
    groups = 4
    rows = groups * SUBLANES
    tiles = tn // LANES

    @pl.when(jnp.logical_and(pl.program_id(0) == 0, pl.program_id(1) == 0))
    def _():
        cv = c_ref[...]
        cond[...] = jnp.broadcast_to(cv * _sigmoid(cv), cond.shape)

    def body(i, accs):
        r = pl.multiple_of(i * rows, rows)
        out = []
        for g in range(groups):
            rg = r + g * SUBLANES
            cv = cond[pl.ds(rg, SUBLANES), :]
            for ti in range(tiles):
                out.append(accs[g * tiles + ti] + w_ref[pl.ds(rg, SUBLANES), ti * LANES:(ti + 1) * LANES] * cv)
        return tuple(out)

    zero = jnp.zeros((SUBLANES, LANES), F32)
    accs = lax.fori_loop(0, d // rows, body, tuple(zero for _ in range(groups * tiles)))
    for ti in range(tiles):
        acc = (accs[ti] + accs[tiles + ti]) + (accs[2 * tiles + ti] + accs[3 * tiles + ti])
        cols = slice(ti * LANES, (ti + 1) * LANES)
        o_ref[:, cols] = jnp.sum(acc, axis=0, keepdims=True) + b_ref[:, cols]


def _adaln(c, ada_w, ada_b):
    n_layers, d, n = ada_w.shape
    tn = 1024
    return pl.pallas_call(
        _adaln_kernel,
        grid=(n_layers, n // tn),
        in_specs=[pl.BlockSpec((d, 1), lambda l, j: (0, 0)),
                  pl.BlockSpec((None, d, tn), lambda l, j: (l, 0, j)),
                  pl.BlockSpec((None, 1, tn), lambda l, j: (l, 0, j))],
        out_specs=pl.BlockSpec((None, 1, tn), lambda l, j: (l, 0, j)),
        out_shape=jax.ShapeDtypeStruct((n_layers, 1, n), F32),
        scratch_shapes=[pltpu.VMEM((d, LANES), F32)],
        compiler_params=_cparams(("arbitrary", "arbitrary")),
        name="adaln",
    )(c.reshape(d, 1), ada_w, ada_b.reshape(n_layers, 1, n))


def _rope_kernel(pos_ref, invf_ref, cos_ref, sin_ref):
    ang = pos_ref[...].astype(F32) * invf_ref[...]
    cos_ref[...] = jnp.cos(ang)
    s = jnp.sin(ang)
    lane = lax.broadcasted_iota(jnp.int32, s.shape, 1)
    sin_ref[...] = jnp.where(lane < HEAD_DIM // 2, -s, s)


def _rope_tables(pos_col):
    t = pos_col.shape[0]
    tm = min(t, 1024)
    inv = ROPE_THETA ** (-jnp.arange(0, HEAD_DIM, 2, dtype=F32) / HEAD_DIM)
    inv = jnp.concatenate([inv, inv]).reshape(1, HEAD_DIM)
    return pl.pallas_call(
        _rope_kernel,
        grid=(t // tm,),
        in_specs=[pl.BlockSpec((tm, 1), lambda i: (i, 0)),
                  pl.BlockSpec((1, HEAD_DIM), lambda i: (0, 0))],
        out_specs=[pl.BlockSpec((tm, HEAD_DIM), lambda i: (i, 0))] * 2,
        out_shape=[jax.ShapeDtypeStruct((t, HEAD_DIM), F32)] * 2,
        compiler_params=_cparams(("arbitrary",)),
        name="rope_tables",
    )(pos_col, inv)


def _inproj_kernel(x_ref, g_ref, sc_ref, sh_ref, w_ref, cos_ref, sin_ref, o_ref, h_scr,
                   *, q_tiles, rope_tiles):
    j = pl.program_id(1)
    tm, tn = o_ref.shape
    rows = 32

    @pl.when(j == 0)
    def _():
        a = g_ref[...] * (1.0 + sc_ref[...])
        b = sh_ref[...]

        def body(i, _):
            r = pl.multiple_of(i * rows, rows)
            xv = x_ref[pl.ds(r, rows), :]
            ms = jnp.mean(xv * xv, axis=-1, keepdims=True)
            h_scr[pl.ds(r, rows), :] = (xv * lax.rsqrt(ms + NORM_EPS) * a + b).astype(BF16)
            return 0

        lax.fori_loop(0, tm // rows, body, 0)

    @pl.when(j < rope_tiles)
    def _():
        acc = jnp.dot(h_scr[...], w_ref[...], preferred_element_type=F32)
        scale = jnp.where(j < q_tiles, HEAD_DIM ** -0.5, 1.0).astype(F32)
        cs = cos_ref[...] * scale
        sn = sin_ref[...] * scale
        for c in range(tn // HEAD_DIM):
            cols = slice(c * HEAD_DIM, (c + 1) * HEAD_DIM)
            t = acc[:, cols]
            o_ref[:, cols] = (t * cs + pltpu.roll(t, HEAD_DIM // 2, axis=1) * sn).astype(BF16)

    @pl.when(j >= rope_tiles)
    def _():
        o_ref[...] = jnp.dot(h_scr[...], w_ref[...], preferred_element_type=F32).astype(BF16)


def _inproj(x, g, sc, sh, w_bf16, cos, sin, attn_width):
    t, d = x.shape
    n = w_bf16.shape[1]
    tm, tn = min(t, 1024), 512
    kern = functools.partial(_inproj_kernel, q_tiles=attn_width // tn, rope_tiles=2 * attn_width // tn)
    vec = pl.BlockSpec((1, d), lambda i, j: (0, 0))
    return pl.pallas_call(
        kern,
        grid=(t // tm, n // tn),
        in_specs=[pl.BlockSpec((tm, d), lambda i, j: (i, 0)), vec, vec, vec,
                  pl.BlockSpec((d, tn), lambda i, j: (0, j)),
                  pl.BlockSpec((tm, HEAD_DIM), lambda i, j: (i, 0)),
                  pl.BlockSpec((tm, HEAD_DIM), lambda i, j: (i, 0))],
        out_specs=pl.BlockSpec((tm, tn), lambda i, j: (i, j)),
        out_shape=jax.ShapeDtypeStruct((t, n), BF16),
        scratch_shapes=[pltpu.VMEM((tm, d), BF16)],
        compiler_params=_cparams(("arbitrary", "arbitrary")),
        name="inproj",
    )(x, g, sc, sh, w_bf16, cos, sin)


ATT_TILE = 2048
ATT_HEADS = 2
ATT_UNROLL = 4


def _rows(start, n, stride):
    return pl.ds(start, n) if stride == 1 else pl.ds(start, n, stride=stride)


def _attn_kernel(q_ref, kc_ref, kp_ref, vc_ref, vp_ref, o_ref, qf, kf, vf, oacc, lacc, *, dilations):
    m = pl.program_id(1)
    tp, wcols = q_ref.shape
    nh = wcols // HEAD_DIM
    w = Q_BLOCK
    n_blk = tp // w

    chunk = 256

    def widen(i, _):
        r = pl.multiple_of(i * chunk, chunk)
        for h in range(nh):
            cols = slice(h * HEAD_DIM, (h + 1) * HEAD_DIM)
            qf[h, pl.ds(r, chunk), :] = q_ref[pl.ds(r, chunk), cols].astype(F32)
            kf[h, pl.ds(r, chunk), :] = kp_ref[pl.ds(r, chunk), cols].astype(F32)
            kf[h, pl.ds(tp + r, chunk), :] = kc_ref[pl.ds(r, chunk), cols].astype(F32)
            vf[h, pl.ds(r, chunk), :] = vp_ref[pl.ds(r, chunk), cols].astype(F32)
            vf[h, pl.ds(tp + r, chunk), :] = vc_ref[pl.ds(r, chunk), cols].astype(F32)
        return 0

    lax.fori_loop(0, tp // chunk, widen, 0)

    qi = lax.broadcasted_iota(jnp.int32, (w, 2 * w), 0)
    kk = lax.broadcasted_iota(jnp.int32, (w, 2 * w), 1)
    band = jnp.logical_and(kk >= qi, kk <= qi + w)
    behind = kk >= w

    def block(bi, d, start, padded):
        start_k = start + tp - w * d
        for h in range(nh):
            q = qf[h, _rows(start, w, d), :].astype(BF16)
            k = kf[h, _rows(start_k, 2 * w, d), :].astype(BF16)
            v = vf[h, _rows(start_k, 2 * w, d), :].astype(BF16)
            s = lax.dot_general(q, k, (((1,), (1,)), ((), ())), preferred_element_type=F32)
            valid = jnp.logical_and(band, jnp.logical_or(behind, jnp.logical_not(padded)))
            s = jnp.where(valid, s, NEG_INF)
            mx = jnp.max(s, axis=-1, keepdims=True)
            p = jnp.exp(s - mx)
            den = jnp.sum(p, axis=-1, keepdims=True)
            o = jnp.dot(p.astype(BF16), v, preferred_element_type=F32) / den
            oacc[bi * nh + h, _rows(start, w, d), :] = o
            lacc[bi * nh + h, _rows(start, w, d), :] = jnp.broadcast_to(mx + jnp.log(den), (w, LANES))

    for bi, d in enumerate(dilations):
        per_class = n_blk // d

        def group(it, _, bi=bi, d=d, per_class=per_class):
            for j in range(ATT_UNROLL):
                idx = it * ATT_UNROLL + j
                r = idx // per_class
                nq = idx % per_class
                start = nq * (d * w) + r
                if d == 1:
                    start = pl.multiple_of(start, w)
                block(bi, d, start, jnp.logical_and(m == 0, nq == 0))
            return 0

        lax.fori_loop(0, n_blk // ATT_UNROLL, group, 0)

    rows = 64

    def mix(i, _):
        r = pl.multiple_of(i * rows, rows)
        for h in range(nh):
            cols = slice(h * HEAD_DIM, (h + 1) * HEAD_DIM)
            ls = [lacc[bi * nh + h, pl.ds(r, rows), :] for bi in range(len(dilations))]
            mx = functools.reduce(jnp.maximum, ls)
            es = [jnp.exp(l - mx) for l in ls]
            inv = 1.0 / functools.reduce(jnp.add, es)
            out = es[0] * inv * oacc[h, pl.ds(r, rows), :]
            for bi in range(1, len(dilations)):
                out = out + es[bi] * inv * oacc[bi * nh + h, pl.ds(r, rows), :]
            o_ref[pl.ds(r, rows), cols] = out.astype(o_ref.dtype)
        return 0

    lax.fori_loop(0, tp // rows, mix, 0)


def _attention(z, attn_width):
    t = z.shape[0]
    tp = ATT_TILE
    dilations = tuple(d for _, d in DILATED_BRANCHES)
    assert all(win // d == Q_BLOCK and win <= tp for win, d in DILATED_BRANCHES) and t % tp == 0
    wcols = ATT_HEADS * HEAD_DIM
    ng = attn_width // wcols

    def cur(off):
        return pl.BlockSpec((tp, wcols), lambda g, m: (m, off * ng + g))

    def prev(off):
        return pl.BlockSpec((tp, wcols), lambda g, m: (jnp.maximum(m - 1, 0), off * ng + g))

    nb = len(dilations)
    return pl.pallas_call(
        functools.partial(_attn_kernel, dilations=dilations),
        grid=(ng, t // tp),
        in_specs=[cur(0), cur(1), prev(1), cur(2), prev(2)],
        out_specs=pl.BlockSpec((tp, wcols), lambda g, m: (m, g)),
        out_shape=jax.ShapeDtypeStruct((t, attn_width), BF16),
        scratch_shapes=[pltpu.VMEM((ATT_HEADS, tp, HEAD_DIM), F32), pltpu.VMEM((ATT_HEADS, 2 * tp, HEAD_DIM), F32),
                        pltpu.VMEM((ATT_HEADS, 2 * tp, HEAD_DIM), F32),
                        pltpu.VMEM((nb * ATT_HEADS, tp, HEAD_DIM), F32),
                        pltpu.VMEM((nb * ATT_HEADS, tp, LANES), F32)],
        compiler_params=_cparams(("arbitrary", "arbitrary")),
        name="attention",
    )(z, z, z, z, z)


def _softplus(x):
    return jnp.maximum(x, 0.0) + jnp.log1p(jnp.exp(-jnp.abs(x)))


def _gelu_tanh(x):
    return 0.5 * x * (1.0 + jnp.tanh(0.7978845608028654 * (x + 0.044715 * x * x * x)))


def _lru_kernel(xr_ref, gr_ref, pos_ref, cw_ref, cb_ref, wg_ref, ba_ref, bx_ref, lam_ref, g_ref,
                o_ref, xbuf, a_scr, b_scr, hcar, *, n_blocks):
    i = pl.program_id(0)
    tc, width = a_scr.shape
    pad = SUBLANES
    bw = LRU_BLOCK_W

    @pl.when(i == 0)
    def _():
        xbuf[0:pad, :] = jnp.zeros((pad, width), F32)
        hcar[...] = jnp.zeros(hcar.shape, F32)

    xbuf[pad:, :] = xr_ref[...].astype(F32)
    reset = pos_ref[...] == 0
    sub = lax.broadcasted_iota(jnp.int32, (tc, bw), 0) % SUBLANES

    for hb in range(n_blocks):
        cols = slice(hb * bw, (hb + 1) * bw)
        xc = cb_ref[:, cols] + cw_ref[0:1, cols] * xbuf[pad - 3:pad - 3 + tc, cols]
        for k in range(1, CONV_WIDTH):
            xc = xc + cw_ref[k:k + 1, cols] * xbuf[pad - 3 + k:pad - 3 + k + tc, cols]
        gates = jnp.dot(xc.astype(BF16), wg_ref[hb], preferred_element_type=F32)
        r = _sigmoid(gates[:, :bw] + ba_ref[:, cols])
        ig = _sigmoid(gates[:, bw:] + bx_ref[:, cols])
        log_a = -LRU_C * r * _softplus(-lam_ref[:, cols])
        ea = jnp.exp(log_a)
        a = jnp.where(reset, 0.0, ea)
        mult = jnp.where(reset, 1.0, jnp.sqrt(-jnp.tanh(log_a) * (ea * ea + 1.0)))
        b = xc * ig * mult
        for s in (1, 2, 4):
            a_s = pltpu.roll(a, s, axis=0)
            b_s = pltpu.roll(b, s, axis=0)
            keep = sub >= s
            b = jnp.where(keep, a * b_s + b, b)
            a = jnp.where(keep, a * a_s, a)
        a_scr[:, cols] = a
        b_scr[:, cols] = b

    def group(gi, h_in):
        r0 = pl.multiple_of(gi * SUBLANES, SUBLANES)
        hh = a_scr[pl.ds(r0, SUBLANES), :] * h_in + b_scr[pl.ds(r0, SUBLANES), :]
        a_scr[pl.ds(r0, SUBLANES), :] = hh
        return jnp.broadcast_to(hh[SUBLANES - 1:SUBLANES, :], hh.shape)

    hcar[...] = lax.fori_loop(0, tc // SUBLANES, group, hcar[...])
    xbuf[0:pad, :] = xbuf[tc:tc + pad, :]

    rows = 32

    def epilogue(si, _):
        r0 = pl.multiple_of(si * rows, rows)
        y = a_scr[pl.ds(r0, rows), :] * _gelu_tanh(gr_ref[pl.ds(r0, rows), :].astype(F32))
        ms = jnp.mean(y * y, axis=-1, keepdims=True)
        o_ref[pl.ds(r0, rows), :] = (y * lax.rsqrt(ms + NORM_EPS) * g_ref[...]).astype(BF16)
        return 0

    lax.fori_loop(0, tc // rows, epilogue, 0)


def _lru(z, pos_col, conv_w, conv_b, wgate_bf16, ba, bx, lam, g, col_block):
    t = z.shape[0]
    width = conv_w.shape[1]
    tc = min(t, 128)
    n_blocks = width // LRU_BLOCK_W
    vec = pl.BlockSpec((1, width), lambda i: (0, 0))
    return pl.pallas_call(
        functools.partial(_lru_kernel, n_blocks=n_blocks),
        grid=(t // tc,),
        in_specs=[pl.BlockSpec((tc, width), lambda i: (i, col_block)),
                  pl.BlockSpec((tc, width), lambda i: (i, col_block + 1)),
                  pl.BlockSpec((tc, 1), lambda i: (i, 0)),
                  pl.BlockSpec((CONV_WIDTH, width), lambda i: (0, 0)), vec,
                  pl.BlockSpec((n_blocks, LRU_BLOCK_W, 2 * LRU_BLOCK_W), lambda i: (0, 0, 0)),
                  vec, vec, vec, vec],
        out_specs=pl.BlockSpec((tc, width), lambda i: (i, 0)),
        out_shape=jax.ShapeDtypeStruct((t, width), BF16),
        scratch_shapes=[pltpu.VMEM((tc + SUBLANES, width), F32), pltpu.VMEM((tc, width), F32),
                        pltpu.VMEM((tc, width), F32), pltpu.VMEM((SUBLANES, width), F32)],
        compiler_params=_cparams(("arbitrary",)),
        name="rg_lru",
    )(z, z, pos_col, conv_w, conv_b, wgate_bf16, ba, bx, lam, g)


def _outproj_kernel(a_ref, ag_ref, r_ref, wa_ref, wr_ref, x_ref, g1_ref, n2_ref, sc_ref, sh_ref, rw_ref, rb_ref,
                    x1_ref, h_ref, ti_ref, tg_ref, *, n_experts):
    a = a_ref[...].astype(F32)
    a = a * lax.rsqrt(jnp.mean(a * a, axis=-1, keepdims=True) + NORM_EPS) * ag_ref[...]
    y = jnp.dot(a.astype(BF16), wa_ref[...], preferred_element_type=F32)
    y = y + jnp.dot(r_ref[...], wr_ref[...], preferred_element_type=F32)
    x1 = x_ref[...] + g1_ref[...] * y
    x1_ref[...] = x1
    ms = jnp.mean(x1 * x1, axis=-1, keepdims=True)
    h = x1 * lax.rsqrt(ms + NORM_EPS) * (n2_ref[...] * (1.0 + sc_ref[...])) + sh_ref[...]
    h_ref[...] = h
    h_hi = h.astype(BF16)
    h_lo = (h - h_hi.astype(F32)).astype(BF16)
    p1 = jnp.dot(h_hi, rw_ref[...], preferred_element_type=F32)
    p2 = jnp.dot(h_lo, rw_ref[:, 0:LANES], preferred_element_type=F32)
    logits = p1[:, 0:LANES] + p1[:, LANES:] + p2 + rb_ref[...]
    lane = lax.broadcasted_iota(jnp.int32, logits.shape, 1).astype(F32)
    cur = jnp.where(lane < n_experts, logits, -jnp.inf)
    vals, idxs = [], []
    for _ in range(TOP_K):
        m = jnp.max(cur, axis=-1, keepdims=True)
        idx = jnp.min(jnp.where(cur == m, lane, float(LANES)), axis=-1, keepdims=True)
        vals.append(m)
        idxs.append(idx)
        cur = jnp.where(lane == idx, -jnp.inf, cur)
    es = [jnp.exp(v - vals[0]) for v in vals]
    inv = 1.0 / (es[0] + es[1] + es[2] + es[3])
    ti = jnp.zeros(logits.shape, F32)
    tg = jnp.zeros(logits.shape, F32)
    for k in range(TOP_K):
        ti = jnp.where(lane == k, idxs[k], ti)
        tg = jnp.where(lane == k, es[k] * inv, tg)
    ti_ref[...] = ti
    tg_ref[...] = tg


def _outproj(attn, attn_g, lru_n, wa_bf16, wr_bf16, x, g1, n2, sc, sh, rw_split, rb_pad, n_experts):
    t, d = x.shape
    wa = attn.shape[1]
    wr = lru_n.shape[1]
    tm = min(t, 512)
    vec = pl.BlockSpec((1, d), lambda i: (0, 0))
    lanes = pl.BlockSpec((tm, LANES), lambda i: (i, 0))
    return pl.pallas_call(
        functools.partial(_outproj_kernel, n_experts=n_experts),
        grid=(t // tm,),
        in_specs=[pl.BlockSpec((tm, wa), lambda i: (i, 0)), pl.BlockSpec((1, wa), lambda i: (0, 0)),
                  pl.BlockSpec((tm, wr), lambda i: (i, 0)),
                  pl.BlockSpec((wa, d), lambda i: (0, 0)), pl.BlockSpec((wr, d), lambda i: (0, 0)),
                  pl.BlockSpec((tm, d), lambda i: (i, 0)), vec, vec, vec, vec,
                  pl.BlockSpec((d, 2 * LANES), lambda i: (0, 0)), pl.BlockSpec((1, LANES), lambda i: (0, 0))],
        out_specs=[pl.BlockSpec((tm, d), lambda i: (i, 0)), pl.BlockSpec((tm, d), lambda i: (i, 0)),
                   lanes, lanes],
        out_shape=[jax.ShapeDtypeStruct((t, d), F32), jax.ShapeDtypeStruct((t, d), F32),
                   jax.ShapeDtypeStruct((t, LANES), F32), jax.ShapeDtypeStruct((t, LANES), F32)],
        compiler_params=_cparams(("arbitrary",)),
        name="outproj",
    )(attn, attn_g, lru_n, wa_bf16, wr_bf16, x, g1, n2, sc, sh, rw_split, rb_pad)


def _route_kernel(ti_ref, dest_ref, cnt_ref, carry, gstart):
    ph = pl.program_id(0)
    i = pl.program_id(1)
    last = pl.num_programs(1) - 1
    tm = ti_ref.shape[0]
    ti = ti_ref[...]
    lane = lax.broadcasted_iota(jnp.int32, (tm, LANES), 1).astype(F32)
    sel = [lane == ti[:, k:k + 1] for k in range(TOP_K)]
    hot = jnp.zeros((tm, LANES), F32)
    for k in range(TOP_K):
        hot = jnp.where(sel[k], 1.0, hot)

    @pl.when(jnp.logical_and(ph == 0, i == 0))
    def _():
        carry[...] = jnp.zeros(carry.shape, F32)

    @pl.when(ph == 1)
    def _():
        row = lax.broadcasted_iota(jnp.int32, (tm, tm), 0)
        col = lax.broadcasted_iota(jnp.int32, (tm, tm), 1)
        tri = jnp.where(col < row, 1.0, 0.0).astype(BF16)
        before = jnp.dot(tri, hot.astype(BF16), preferred_element_type=F32) + carry[0:1, :]
        base = before + gstart[0:1, :]
        dest = jnp.zeros((tm, LANES), F32)
        for k in range(TOP_K):
            dk = jnp.sum(jnp.where(sel[k], base, 0.0), axis=-1, keepdims=True)
            dest = jnp.where(lane == k, dk, dest)
        dest_ref[...] = dest.astype(jnp.int32)

    carry[...] = carry[...] + jnp.sum(hot, axis=0, keepdims=True)

    @pl.when(jnp.logical_and(ph == 0, i == last))
    def _():
        cnt = carry[...]
        cnt_ref[...] = cnt[0:1, :]
        padded = jnp.ceil(cnt / MOE_SUB) * MOE_SUB
        lane8 = lax.broadcasted_iota(jnp.int32, cnt.shape, 1)
        acc = padded
        for s in (1, 2, 4, 8, 16, 32, 64):
            acc = acc + jnp.where(lane8 >= s, pltpu.roll(acc, s, axis=1), 0.0)
        gstart[...] = acc - padded
        carry[...] = jnp.zeros(carry.shape, F32)


def _route(topi):
    t = topi.shape[0]
    tm = min(t, TOK_TILE)
    dest, cnt = pl.pallas_call(
        _route_kernel,
        grid=(2, t // tm),
        in_specs=[pl.BlockSpec((tm, LANES), lambda p, i: (i, 0))],
        out_specs=[pl.BlockSpec((tm, LANES), lambda p, i: (i * p, 0)),
                   pl.BlockSpec((1, LANES), lambda p, i: (0, 0))],
        out_shape=[jax.ShapeDtypeStruct((t, LANES), jnp.int32), jax.ShapeDtypeStruct((1, LANES), F32)],
        scratch_shapes=[pltpu.VMEM((SUBLANES, LANES), F32), pltpu.VMEM((SUBLANES, LANES), F32)],
        compiler_params=_cparams(("arbitrary", "arbitrary")),
        name="route",
    )(topi)
    return dest, cnt


def _row_copy(src, dst, sem):
    return pltpu.make_async_copy(src, dst, sem)


def _dispatch_kernel(dest_sm, h_ref, xs_ref, sem):
    i = pl.program_id(0)
    tm = h_ref.shape[0]

    def issue(r, _):
        for k in range(TOP_K):
            d = dest_sm[(i * tm + r) * TOP_K + k]
            _row_copy(h_ref.at[pl.ds(r, 1), :], xs_ref.at[pl.ds(d, 1), :], sem).start(priority=k % 2)
        return 0

    lax.fori_loop(0, tm, issue, 0)
    for _ in range(TOP_K):
        _row_copy(h_ref, xs_ref.at[pl.ds(0, tm), :], sem).wait()


def _dispatch(dest_flat, h, n_rows):
    t, d = h.shape
    tm = min(t, TOK_TILE)
    return pl.pallas_call(
        _dispatch_kernel,
        grid_spec=pltpu.PrefetchScalarGridSpec(
            num_scalar_prefetch=1,
            grid=(t // tm,),
            in_specs=[pl.BlockSpec((tm, d), lambda i, dd: (i, 0))],
            out_specs=pl.BlockSpec(memory_space=pl.ANY),
            scratch_shapes=[pltpu.SemaphoreType.DMA],
        ),
        out_shape=jax.ShapeDtypeStruct((n_rows, d), F32),
        compiler_params=_cparams(("arbitrary",)),
        name="dispatch",
    )(dest_flat, h)


TAIL_PIECES = tuple(MOE_SUB >> (i + 1) for i in range(MOE_SUB.bit_length() - 1))


def _moe_kernel(ue_sm, ur_sm, ub_sm, na_sm, xs_ref, w1_ref, b1_ref, w2_ref, b2_ref, sel_ref, ys_ref,
                xbuf, acc, xsem, ysem):
    del ue_sm
    u = pl.program_id(0)
    f = pl.program_id(1)
    last_f = pl.num_programs(1) - 1
    n_act = na_sm[0]
    d = xbuf.shape[1]
    tf2 = w1_ref.shape[1]

    @pl.when(jnp.logical_and(u == 0, f == 0))
    def _():
        xbuf[...] = jnp.zeros(xbuf.shape, F32)

    def x_copy(base, off, n, slot):
        return pltpu.make_async_copy(xs_ref.at[pl.ds(base + off, n), :], xbuf.at[pl.ds(off, n), :], xsem.at[slot])

    def y_copy(base, bi):
        r0 = pl.multiple_of(bi * MOE_SUB, MOE_SUB)
        return pltpu.make_async_copy(acc.at[pl.ds(r0, MOE_SUB), :], ys_ref.at[pl.ds(base + r0, MOE_SUB), :],
                                     ysem.at[bi])

    def tail_pieces(base, n_full, rem, fn):
        off = n_full * MOE_SUB
        for p in TAIL_PIECES:
            take = (rem & p) != 0

            @pl.when(take)
            def _(off=off, p=p):
                fn(x_copy(base, pl.multiple_of(off, p), p, n_full))

            off = off + jnp.where(take, p, 0)

    @pl.when(u < n_act)
    def _():
        rows = ur_sm[u]
        base = pl.multiple_of(ub_sm[u], MOE_SUB)
        n_full = rows // MOE_SUB
        rem = rows % MOE_SUB
        n_sub = (rows + MOE_SUB - 1) // MOE_SUB

        @pl.when(f == 0)
        def _():
            @pl.when(u > 0)
            def _():
                prev_base = pl.multiple_of(ub_sm[u - 1], MOE_SUB)
                prev_sub = (ur_sm[u - 1] + MOE_SUB - 1) // MOE_SUB

                def drain(bi, _):
                    y_copy(prev_base, bi).wait()
                    return 0

                lax.fori_loop(0, prev_sub, drain, 0)

            def fetch(bi, _):
                x_copy(base, pl.multiple_of(bi * MOE_SUB, MOE_SUB), MOE_SUB, bi).start()
                return 0

            lax.fori_loop(0, n_full, fetch, 0)
            tail_pieces(base, n_full, rem, lambda c: c.start())

        def wait_x(g):
            @pl.when(g < n_full)
            def _():
                x_copy(base, pl.multiple_of(g * MOE_SUB, MOE_SUB), MOE_SUB, g).wait()

            @pl.when(g >= n_full)
            def _():
                tail_pieces(base, n_full, rem, lambda c: c.wait())

        def compute(r0, m, subs):
            @pl.when(f == 0)
            def _():
                for g in subs:
                    wait_x(g)
                acc[pl.ds(r0, m), :] = jnp.broadcast_to(b2_ref[...], (m, d))

            xb = xbuf[pl.ds(r0, m), :].astype(BF16)
            uu = jnp.dot(xb, w1_ref[...].astype(BF16), preferred_element_type=F32) + b1_ref[...]
            glu = jnp.minimum(uu, SWIGLU_LIMIT)
            glu = glu * _sigmoid(SWIGLU_ALPHA * glu)
            lin = jnp.clip(uu, -SWIGLU_LIMIT, SWIGLU_LIMIT) + 1.0
            acts = []
            for cidx in range(tf2 // (2 * LANES)):
                cs = slice(cidx * 2 * LANES, (cidx + 1) * 2 * LANES)
                prod = glu[:, cs] * pltpu.roll(lin[:, cs], 2 * LANES - 1, axis=1)
                acts.append(jnp.dot(prod.astype(BF16), sel_ref[...], preferred_element_type=F32))
            act = jnp.concatenate(acts, axis=1).astype(BF16)
            acc[pl.ds(r0, m), :] += jnp.dot(act, w2_ref[...].astype(BF16), preferred_element_type=F32)

            @pl.when(f == last_f)
            def _():
                for g in subs:
                    y_copy(base, g).start()

        def pair(bi, _):
            compute(pl.multiple_of(bi * MOE_PAIR, MOE_PAIR), MOE_PAIR, (2 * bi, 2 * bi + 1))
            return 0

        lax.fori_loop(0, n_sub // 2, pair, 0)

        @pl.when(n_sub % 2 == 1)
        def _():
            compute(pl.multiple_of((n_sub - 1) * MOE_SUB, MOE_SUB), MOE_SUB, (n_sub - 1,))

        @pl.when(jnp.logical_and(f == last_f, u == n_act - 1))
        def _():
            def drain(bi, _):
                y_copy(base, bi).wait()
                return 0

            lax.fori_loop(0, n_sub, drain, 0)


def _moe(xs, w1, b1, w2, b2, layer, unit_expert, unit_rows, unit_base, n_active):
    n_rows, d = xs.shape
    n_layers, n_exp, _, ff2 = w1.shape
    ff = ff2 // 2
    n_units = unit_expert.shape[0]
    nf = ff // MOE_TF
    n_slots = MOE_UNIT // MOE_SUB
    sel = (jnp.arange(2 * LANES)[:, None] == 2 * jnp.arange(LANES)[None, :]).astype(BF16)

    def fstep(u, f, na):
        return jnp.where(u < na[0], f, nf - 1)

    return pl.pallas_call(
        _moe_kernel,
        grid_spec=pltpu.PrefetchScalarGridSpec(
            num_scalar_prefetch=4,
            grid=(n_units, nf),
            in_specs=[
                pl.BlockSpec(memory_space=pl.ANY),
                pl.BlockSpec((None, None, d, 2 * MOE_TF),
                             lambda u, f, ue, ur, ub, na: (layer, ue[u], 0, fstep(u, f, na))),
                pl.BlockSpec((None, None, 1, 2 * MOE_TF),
                             lambda u, f, ue, ur, ub, na: (layer, ue[u], 0, fstep(u, f, na))),
                pl.BlockSpec((None, None, MOE_TF, d),
                             lambda u, f, ue, ur, ub, na: (layer, ue[u], fstep(u, f, na), 0)),
                pl.BlockSpec((None, None, 1, d), lambda u, f, ue, ur, ub, na: (layer, ue[u], 0, 0)),
                pl.BlockSpec((2 * LANES, LANES), lambda u, f, ue, ur, ub, na: (0, 0)),
            ],
            out_specs=pl.BlockSpec(memory_space=pl.ANY),
            scratch_shapes=[pltpu.VMEM((MOE_UNIT, d), F32), pltpu.VMEM((MOE_UNIT, d), F32),
                            pltpu.SemaphoreType.DMA((n_slots,)), pltpu.SemaphoreType.DMA((n_slots,))],
        ),
        out_shape=jax.ShapeDtypeStruct((n_rows, d), F32),
        compiler_params=_cparams(("arbitrary", "arbitrary")),
        name="moe_experts",
    )(unit_expert, unit_rows, unit_base, n_active, xs, w1, b1.reshape(n_layers, n_exp, 1, ff2), w2,
      b2.reshape(n_layers, n_exp, 1, d), sel)


def _combine_kernel(dest_sm, ys_ref, tg_ref, x1_ref, g2_ref, fg_ref, o_ref, buf, sem, *, final_norm):
    i = pl.program_id(0)
    tm = x1_ref.shape[0]

    def issue(r, _):
        for k in range(TOP_K):
            d = dest_sm[(i * tm + r) * TOP_K + k]
            _row_copy(ys_ref.at[pl.ds(d, 1), :], buf.at[k, pl.ds(r, 1), :], sem).start(priority=k % 2)
        return 0

    lax.fori_loop(0, tm, issue, 0)
    for k in range(TOP_K):
        _row_copy(ys_ref.at[pl.ds(0, tm), :], buf.at[k], sem).wait()

    rows = 32

    def body(si, _):
        r0 = pl.multiple_of(si * rows, rows)
        tg = tg_ref[pl.ds(r0, rows), :]
        y = tg[:, 0:1] * buf[0, pl.ds(r0, rows), :]
        for k in range(1, TOP_K):
            y = y + tg[:, k:k + 1] * buf[k, pl.ds(r0, rows), :]
        x2 = x1_ref[pl.ds(r0, rows), :] + g2_ref[...] * y
        if final_norm:
            ms = jnp.mean(x2 * x2, axis=-1, keepdims=True)
            x2 = x2 * lax.rsqrt(ms + NORM_EPS) * fg_ref[...]
        o_ref[pl.ds(r0, rows), :] = x2
        return 0

    lax.fori_loop(0, tm // rows, body, 0)


def _combine(dest_flat, ys, tgate, x1, g2, final_g, final_norm):
    t, d = x1.shape
    tm = min(t, TOK_TILE)
    vec = pl.BlockSpec((1, d), lambda i, dd: (0, 0))
    return pl.pallas_call(
        functools.partial(_combine_kernel, final_norm=final_norm),
        grid_spec=pltpu.PrefetchScalarGridSpec(
            num_scalar_prefetch=1,
            grid=(t // tm,),
            in_specs=[pl.BlockSpec(memory_space=pl.ANY),
                      pl.BlockSpec((tm, LANES), lambda i, dd: (i, 0)),
                      pl.BlockSpec((tm, d), lambda i, dd: (i, 0)), vec, vec],
            out_specs=pl.BlockSpec((tm, d), lambda i, dd: (i, 0)),
            scratch_shapes=[pltpu.VMEM((TOP_K, tm, d), F32), pltpu.SemaphoreType.DMA],
        ),
        out_shape=jax.ShapeDtypeStruct((t, d), F32),
        compiler_params=_cparams(("arbitrary",)),
        name="moe_combine",
    )(dest_flat, ys, tgate, x1, g2, final_g)


def _unit_tables(counts, n_units):
    n_exp = counts.shape[0]
    padded = (counts + MOE_SUB - 1) // MOE_SUB * MOE_SUB
    gstart = jnp.cumsum(padded) - padded
    units_per = (counts + MOE_UNIT - 1) // MOE_UNIT
    ends = jnp.cumsum(units_per)
    starts = ends - units_per
    n_active = ends[-1]
    u = jnp.arange(n_units, dtype=jnp.int32)
    uc = jnp.minimum(u, n_active - 1)
    ue = jnp.minimum(jnp.sum((ends[None, :] <= uc[:, None]).astype(jnp.int32), axis=1), n_exp - 1)
    part = uc - starts[ue]
    rows = jnp.where(u < n_active, jnp.clip(counts[ue] - part * MOE_UNIT, 0, MOE_UNIT), 0)
    base = gstart[ue] + part * MOE_UNIT
    return ue.astype(jnp.int32), rows.astype(jnp.int32), base.astype(jnp.int32), n_active.reshape(1).astype(jnp.int32)


def kernel(x, c, positions, ada_w, ada_b, norm1_g, norm2_g, w_in, conv_w, conv_b, lru_wa, lru_ba, lru_wx,
           lru_bx, lru_lambda, attn_out_g, lru_out_g, w_out, router_w, router_b, w1, b1, w2, b2, final_g):
    bsz, seq, d = x.shape
    assert bsz == 1
    t = seq
    n_layers = ada_w.shape[0]
    attn_width = attn_out_g.shape[1]
    lru_width = lru_out_g.shape[1]
    n_experts = router_w.shape[2]
    assert attn_width == lru_width and w_in.shape[2] == 3 * attn_width + 2 * lru_width
    n_units = (t * TOP_K) // MOE_UNIT + n_experts
    n_rows = t * TOP_K + n_experts * MOE_SUB

    xf = x.reshape(t, d)
    pos_col = positions.reshape(t, 1)
    mod = _adaln(c, ada_w, ada_b)
    cos, sin = _rope_tables(pos_col)
    fg = final_g.reshape(1, d)

    for l in range(n_layers):
        sh1, sc1, g1, sh2, sc2, g2 = [mod[l, :, i * d:(i + 1) * d] for i in range(6)]
        z = _inproj(xf, norm1_g[l].reshape(1, d), sc1, sh1, w_in[l].astype(BF16), cos, sin, attn_width)
        attn = _attention(z, attn_width)
        wgate = jnp.concatenate([lru_wa[l], lru_wx[l]], axis=-1).astype(BF16)
        lru_n = _lru(z, pos_col, conv_w[l], conv_b[l].reshape(1, -1), wgate, lru_ba[l].reshape(1, -1),
                     lru_bx[l].reshape(1, -1), lru_lambda[l].reshape(1, -1), lru_out_g[l].reshape(1, -1),
                     3 * attn_width // lru_width)
        wo = w_out[l].astype(BF16)
        rw_pad = jnp.pad(router_w[l], ((0, 0), (0, LANES - n_experts)))
        rw_hi = rw_pad.astype(BF16)
        rw_split = jnp.concatenate([rw_hi, (rw_pad - rw_hi.astype(F32)).astype(BF16)], axis=1)
        rb_pad = jnp.pad(router_b[l], (0, LANES - n_experts)).reshape(1, LANES)
        x1, h, topi, tgate = _outproj(attn, attn_out_g[l].reshape(1, attn_width), lru_n, wo[:attn_width],
                                       wo[attn_width:], xf, g1,
                                       norm2_g[l].reshape(1, d), sc2, sh2, rw_split, rb_pad, n_experts)
        dest, cnt = _route(topi)
        counts = cnt[0, :n_experts].astype(jnp.int32)
        unit_expert, unit_rows, unit_base, n_active = _unit_tables(counts, n_units)
        dest_flat = dest[:, :TOP_K].reshape(t * TOP_K)
        xs = _dispatch(dest_flat, h, n_rows)
        ys = _moe(xs, w1, b1, w2, b2, l, unit_expert, unit_rows, unit_base, n_active)
        xf = _combine(dest_flat, ys, tgate, x1, g2, fg, l == n_layers - 1)
    return xf.reshape(bsz, seq, d)
```
